```python
import jax, jax.numpy as jnp
from jax import lax
import numpy as np

D_MODEL = 2048
BATCH = 2
SEQ = 8192
DEPTH = 4

CHUNK = 64
N_MIXERS = 4
SHORT_CONV_WIDTH = 3
POOL_WINDOWS = (2, 4, 8, 16)
POOL_GROUP = D_MODEL // len(POOL_WINDOWS)
CONF_CONV_WIDTH = 31
SB_HEAD_DIM = 128
SB_HEADS = D_MODEL // SB_HEAD_DIM
Q_BLOCK = 128
N_EXPERTS = 32
TOP_K = 4
D_EXPERT = D_MODEL // 2
SWIGLU_LIMIT = 7.0
SWIGLU_ALPHA = 1.702
EXPERT_BLOCK = 128
N_MOD = 6
EPS = 1e-6

kernel_name = "hybrid_streaming_encoder_block"


def n_occurrences(kind):
    return len(range(kind, DEPTH, N_MIXERS))


def rms_norm(x, g):
    xf = x.astype(jnp.float32)
    y = xf * lax.rsqrt(jnp.mean(xf * xf, axis=-1, keepdims=True) + EPS)
    return (y * g.astype(jnp.float32)).astype(x.dtype)


def layer_norm(x, g, b):
    xf = x.astype(jnp.float32)
    mu = jnp.mean(xf, axis=-1, keepdims=True)
    var = jnp.mean(jnp.square(xf - mu), axis=-1, keepdims=True)
    y = (xf - mu) * lax.rsqrt(var + EPS) * g.astype(jnp.float32) + b.astype(jnp.float32)
    return y.astype(x.dtype)


def causal_depthwise_conv(u, w):
    k = w.shape[0]
    return lax.conv_general_dilated(
        u, w[:, None, :].astype(u.dtype), window_strides=(1,), padding=[(k - 1, 0)],
        dimension_numbers=("NWC", "WIO", "NWC"), feature_group_count=u.shape[-1])


def short_conv_mixer(h, w_in, w_conv, w_out):
    b_gate, c_gate, u = jnp.split(h @ w_in, 3, axis=-1)
    return (b_gate * causal_depthwise_conv(c_gate * u, w_conv)) @ w_out


def pooling_mixer(h, w_pool, pool_scale):
    bsz, seq, d = h.shape
    hf = h.astype(jnp.float32)
    csp = jnp.pad(jnp.cumsum(hf, axis=1), ((0, 0), (1, 0), (0, 0)))
    pos = jnp.arange(seq)
    groups = []
    for g, win in enumerate(POOL_WINDOWS):
        cg = csp[..., g * POOL_GROUP:(g + 1) * POOL_GROUP]
        upper = cg[:, 1:]
        lower = jnp.pad(cg, ((0, 0), (win - 1, 0), (0, 0)))[:, :seq]
        count = jnp.minimum(pos + 1, win).astype(jnp.float32)[None, :, None]
        groups.append((upper - lower) / count - hf[..., g * POOL_GROUP:(g + 1) * POOL_GROUP])
    pooled = jnp.stack(groups, axis=2).astype(h.dtype)
    y = jnp.einsum("bsgc,gcd->bsgd", pooled, w_pool).reshape(bsz, seq, d)
    return y * pool_scale


def conformer_conv_mixer(h, w_pw1, b_pw1, w_dw, b_dw, ln_g, ln_b, w_pw2, b_pw2):
    a, g = jnp.split(h @ w_pw1 + b_pw1, 2, axis=-1)
    u = a * jax.nn.sigmoid(g)
    u = causal_depthwise_conv(u, w_dw) + b_dw
    u = jax.nn.silu(layer_norm(u, ln_g, ln_b))
    return u @ w_pw2 + b_pw2


def head_rms_norm(t, g):
    tf = t.astype(jnp.float32)
    y = tf * lax.rsqrt(jnp.mean(tf * tf, axis=-1, keepdims=True) + EPS)
    return (y * g.astype(jnp.float32)).astype(t.dtype)


def stick_breaking_mixer(h, w_qkv, q_gain, k_gain, w_o):
    bsz, seq, d = h.shape
    qkv = (h @ w_qkv).reshape(bsz, seq, 3, SB_HEADS, SB_HEAD_DIM)
    q = head_rms_norm(qkv[:, :, 0], q_gain).transpose(0, 2, 1, 3)
    k = head_rms_norm(qkv[:, :, 1], k_gain).transpose(0, 2, 1, 3)
    v = qkv[:, :, 2].transpose(0, 2, 1, 3)
    n_blk = seq // Q_BLOCK
    q_blocks = q.reshape(bsz, SB_HEADS, n_blk, Q_BLOCK, SB_HEAD_DIM).transpose(2, 0, 1, 3, 4)
    key_pos = jnp.arange(seq)
    scale = SB_HEAD_DIM ** -0.5

    def block(args):
        q_blk, t0 = args
        z = jnp.einsum("bhqd,bhkd->bhqk", q_blk, k).astype(jnp.float32) * scale
        q_pos = t0 + jnp.arange(Q_BLOCK)
        valid = key_pos[None, :] < q_pos[:, None]
        log_1mb = jnp.where(valid, jax.nn.log_sigmoid(-z), 0.0)
        log_after = lax.cumsum(log_1mb, axis=3, reverse=True) - log_1mb
        a = jnp.where(valid, jnp.exp(jax.nn.log_sigmoid(z) + log_after), 0.0)
        return jnp.einsum("bhqk,bhkd->bhqd", a.astype(v.dtype), v)

    o = lax.map(block, (q_blocks, jnp.arange(n_blk, dtype=jnp.int32) * Q_BLOCK))
    o = o.transpose(1, 0, 3, 2, 4).reshape(bsz, seq, d)
    return o @ w_o


def routed_experts(h, router_w, router_b, w_gu, b_gu, w_down, b_down):
    bsz, seq, d = h.shape
    n_tok = bsz * seq
    n_assign = n_tok * TOP_K
    xt = h.reshape(n_tok, d)
    logits = (xt @ router_w).astype(jnp.float32) + router_b.astype(jnp.float32)
    top_logit, top_idx = lax.top_k(logits, TOP_K)
    gate = jax.nn.softmax(top_logit, axis=-1)
    e_flat = top_idx.reshape(-1).astype(jnp.int32)
    tok_flat = jnp.repeat(jnp.arange(n_tok, dtype=jnp.int32), TOP_K)
    order = jnp.argsort(e_flat)
    e_sorted = e_flat[order]
    tok_sorted = tok_flat[order]
    gate_sorted = gate.reshape(-1)[order]
    counts = jnp.bincount(e_flat, length=N_EXPERTS)
    padded = (counts + EXPERT_BLOCK - 1) // EXPERT_BLOCK * EXPERT_BLOCK
    pad_end = jnp.cumsum(padded)
    pad_start = pad_end - padded
    raw_start = jnp.cumsum(counts) - counts
    dest = pad_start[e_sorted] + jnp.arange(n_assign, dtype=jnp.int32) - raw_start[e_sorted]
    n_rows = n_assign + N_EXPERTS * EXPERT_BLOCK
    n_blocks = n_rows // EXPERT_BLOCK
    rows = jnp.zeros((n_rows, d), h.dtype).at[dest].set(xt[tok_sorted])
    block_expert = jnp.minimum(
        jnp.searchsorted(pad_end, jnp.arange(n_blocks, dtype=jnp.int32) * EXPERT_BLOCK, side="right"),
        N_EXPERTS - 1)

    def expert_block(args):
        xb, e = args
        hb = xb @ w_gu[e] + b_gu[e]
        g, u = hb[:, :D_EXPERT], hb[:, D_EXPERT:]
        g = jnp.minimum(g, SWIGLU_LIMIT)
        u = jnp.clip(u, -SWIGLU_LIMIT, SWIGLU_LIMIT)
        act = g * jax.nn.sigmoid(SWIGLU_ALPHA * g) * (u + 1)
        return act @ w_down[e] + b_down[e]

    out_rows = lax.map(expert_block, (rows.reshape(n_blocks, EXPERT_BLOCK, d), block_expert))
    contrib = out_rows.reshape(n_rows, d)[dest] * gate_sorted[:, None].astype(h.dtype)
    return jnp.zeros((n_tok, d), h.dtype).at[tok_sorted].add(contrib).reshape(bsz, seq, d)


def setup_inputs(seed: int = 0) -> dict:
    key = jax.random.key(seed)
    ks = iter(jax.random.split(key, 40))

    def nrm(shape, scale):
        return jax.random.normal(next(ks), shape, jnp.float32) * scale

    def gain(shape):
        return 1.0 + nrm(shape, 0.1)

    D = D_MODEL
    nA, nB, nC, nD = (n_occurrences(kd) for kd in range(N_MIXERS))
    return {
        "x": nrm((BATCH, SEQ, D), 1.0),
        "c": nrm((BATCH, D), 1.0),
        "w_ada": nrm((D, N_MOD * D), 0.5 * D ** -0.5),
        "b_ada": nrm((N_MOD * D,), 0.02),
        "ada_table": nrm((DEPTH, N_MOD, D), 0.1),
        "norm_mix": gain((DEPTH, D)),
        "norm_ffn": gain((DEPTH, D)),
        "sc_w_in": nrm((nA, D, 3 * D), D ** -0.5),
        "sc_w_conv": nrm((nA, SHORT_CONV_WIDTH, D), SHORT_CONV_WIDTH ** -0.5),
        "sc_w_out": nrm((nA, D, D), D ** -0.5),
        "pool_w": nrm((nB, len(POOL_WINDOWS), POOL_GROUP, POOL_GROUP), POOL_GROUP ** -0.5),
        "pool_scale": gain((nB, D)),
        "cf_w_pw1": nrm((nC, D, 2 * D), D ** -0.5),
        "cf_b_pw1": nrm((nC, 2 * D), 0.02),
        "cf_w_dw": nrm((nC, CONF_CONV_WIDTH, D), CONF_CONV_WIDTH ** -0.5),
        "cf_b_dw": nrm((nC, D), 0.02),
        "cf_ln_g": gain((nC, D)),
        "cf_ln_b": nrm((nC, D), 0.02),
        "cf_w_pw2": nrm((nC, D, D), D ** -0.5),
        "cf_b_pw2": nrm((nC, D), 0.02),
        "sb_w_qkv": nrm((nD, D, 3 * D), D ** -0.5),
        "sb_q_gain": gain((nD, SB_HEAD_DIM)),
        "sb_k_gain": gain((nD, SB_HEAD_DIM)),
        "sb_w_o": nrm((nD, D, D), D ** -0.5),
        "router_w": nrm((DEPTH, D, N_EXPERTS), D ** -0.5),
        "router_b": nrm((DEPTH, N_EXPERTS), 0.01),
        "exp_w_gu": nrm((DEPTH, N_EXPERTS, D, 2 * D_EXPERT), D ** -0.5),
        "exp_b_gu": nrm((DEPTH, N_EXPERTS, 2 * D_EXPERT), 0.02),
        "exp_w_down": nrm((DEPTH, N_EXPERTS, D_EXPERT, D), D_EXPERT ** -0.5),
        "exp_b_down": nrm((DEPTH, N_EXPERTS, D), 0.02),
    }


def reference(x, c, w_ada, b_ada, ada_table, norm_mix, norm_ffn,
              sc_w_in, sc_w_conv, sc_w_out,
              pool_w, pool_scale,
              cf_w_pw1, cf_b_pw1, cf_w_dw, cf_b_dw, cf_ln_g, cf_ln_b, cf_w_pw2, cf_b_pw2,
              sb_w_qkv, sb_q_gain, sb_k_gain, sb_w_o,
              router_w, router_b, exp_w_gu, exp_b_gu, exp_w_down, exp_b_down):
    bsz = x.shape[0]
    mod_shared = (jax.nn.silu(c) @ w_ada + b_ada).reshape(bsz, N_MOD, D_MODEL)
    for i in range(DEPTH):
        mod = (mod_shared + ada_table[i][None]).astype(x.dtype)
        sh1, sc1, g1, sh2, sc2, g2 = (mod[:, j, None, :] for j in range(N_MOD))
        h = rms_norm(x, norm_mix[i]) * (1 + sc1) + sh1
        kind, occ = i % N_MIXERS, i // N_MIXERS
        if kind == 0:
            y = short_conv_mixer(h, sc_w_in[occ], sc_w_conv[occ], sc_w_out[occ])
        elif kind == 1:
            y = pooling_mixer(h, pool_w[occ], pool_scale[occ])
        elif kind == 2:
            y = conformer_conv_mixer(h, cf_w_pw1[occ], cf_b_pw1[occ], cf_w_dw[occ], cf_b_dw[occ],
                                     cf_ln_g[occ], cf_ln_b[occ], cf_w_pw2[occ], cf_b_pw2[occ])
        else:
            y = stick_breaking_mixer(h, sb_w_qkv[occ], sb_q_gain[occ], sb_k_gain[occ], sb_w_o[occ])
        x = x + g1 * y
        h = rms_norm(x, norm_ffn[i]) * (1 + sc2) + sh2
        x = x + g2 * routed_experts(h, router_w[i], router_b[i], exp_w_gu[i], exp_b_gu[i],
                                    exp_w_down[i], exp_b_down[i])
    return x
```

```python
import functools

import jax
import jax.numpy as jnp
from jax import lax
from jax.experimental import pallas as pl
from jax.experimental.pallas import tpu as pltpu

EPS = 1e-6
LANES = 128
SUBLANES = 8
HEAD_DIM = 128
POOL_WINDOWS = (2, 4, 8, 16)
POOL_HALO = 16
TOP_K = 4
SWIGLU_LIMIT = 7.0
SWIGLU_ALPHA = 1.702
N_MOD = 6
VMEM_LIMIT_BYTES = 52 * 1024 * 1024
NEG_BIG = -1e30
ROW_TILE = 256
MM_TILE = 512
COL_TILE = 512
EXPERT_TILE = 256
ATTN_TILE = 256
PERMUTE_TOKENS = 256

F32 = jnp.float32
BF16 = jnp.bfloat16


def _params(*sem):
    return pltpu.CompilerParams(dimension_semantics=sem, vmem_limit_bytes=VMEM_LIMIT_BYTES)


def _rms_mod(x, gain, scale, shift):
    ms = jnp.mean(x * x, axis=-1, keepdims=True)
    return x * lax.rsqrt(ms + EPS) * gain * (1.0 + scale) + shift


def _dot(a, b):
    return jnp.dot(a, b, preferred_element_type=F32)


def _pack_rows(val, ref, n_rows):
    half = val.shape[1] // 2
    wr = half // LANES
    bits = pltpu.bitcast(val.astype(BF16).astype(F32), jnp.uint32)
    word = (bits[:, :half] & jnp.uint32(0xFFFF0000)) | (bits[:, half:] >> 16)
    for c in range(wr):
        ref[pl.ds(c, n_rows, stride=wr), :] = word[:, c * LANES:(c + 1) * LANES]


def _unpack_slab(w):
    hi = pltpu.bitcast(w & jnp.uint32(0xFFFF0000), F32)
    lo = pltpu.bitcast(w << 16, F32)
    return hi, lo


def _ada_kernel(c_ref, w_ref, b_ref, t_ref, o_ref):
    c = c_ref[...]
    a = c * jax.nn.sigmoid(c)
    shared = jnp.dot(a, w_ref[...], preferred_element_type=F32,
                     precision=lax.Precision.HIGHEST) + b_ref[...]
    for i in range(o_ref.shape[0]):
        o_ref[i] = shared + t_ref[i:i + 1, :]


def _ada(c, w_ada, b_ada, ada_table):
    bsz, d = c.shape
    depth = ada_table.shape[0]
    n = w_ada.shape[1]
    c8 = jnp.zeros((SUBLANES, d), F32).at[:bsz].set(c)
    tn = min(COL_TILE, n)
    out = pl.pallas_call(
        _ada_kernel,
        out_shape=jax.ShapeDtypeStruct((depth, SUBLANES, n), F32),
        grid=(n // tn,),
        in_specs=[pl.BlockSpec((SUBLANES, d), lambda j: (0, 0)),
                  pl.BlockSpec((d, tn), lambda j: (0, j)),
                  pl.BlockSpec((1, tn), lambda j: (0, j)),
                  pl.BlockSpec((depth, tn), lambda j: (0, j))],
        out_specs=pl.BlockSpec((depth, SUBLANES, tn), lambda j: (0, 0, j)),
        compiler_params=_params("arbitrary"),
        name="ada_mod",
    )(c8, w_ada, b_ada.reshape(1, n), ada_table.reshape(depth, n))
    return out[:, :bsz].reshape(depth, bsz, N_MOD, d)


def _norm_mod_kernel(x_ref, g_ref, mod_ref, o_ref):
    m = mod_ref[0]
    o_ref[...] = _rms_mod(x_ref[...], g_ref[...], m[1:2], m[0:1]).astype(o_ref.dtype)


def _norm_mod(x2, gain, mod, seq):
    n, d = x2.shape
    tm = min(ROW_TILE, seq)
    tpb = seq // tm
    return pl.pallas_call(
        _norm_mod_kernel,
        out_shape=jax.ShapeDtypeStruct((n, d), BF16),
        grid=(n // tm,),
        in_specs=[pl.BlockSpec((tm, d), lambda i: (i, 0)),
                  pl.BlockSpec((1, d), lambda i: (0, 0)),
                  pl.BlockSpec((1, N_MOD, d), lambda i: (i // tpb, 0, 0))],
        out_specs=pl.BlockSpec((tm, d), lambda i: (i, 0)),
        compiler_params=_params("arbitrary"),
        name="norm_mod",
    )(x2, gain.reshape(1, d), mod)


def _causal_taps(val, buf, taps, halo, first):
    tm = val.shape[0]
    k_taps = taps.shape[0]

    @pl.when(first)
    def _():
        buf[0:halo] = jnp.zeros((halo, val.shape[1]), F32)

    @pl.when(jnp.logical_not(first))
    def _():
        buf[0:halo] = buf[tm:tm + halo]

    buf[halo:halo + tm] = val
    acc = taps[k_taps - 1:k_taps] * val
    for k in range(k_taps - 1):
        off = halo - (k_taps - 1) + k
        acc = acc + taps[k:k + 1] * buf[off:off + tm]
    return acc


def _sc_in_kernel(h_ref, wb_ref, wc_ref, wu_ref, cw_ref, o_ref, buf, *, tpb, halo):
    i = pl.program_id(1)
    h = h_ref[...]
    b_gate = _dot(h, wb_ref[...])
    v = _dot(h, wc_ref[...]) * _dot(h, wu_ref[...])
    conv = _causal_taps(v, buf, cw_ref[...], halo, (i % tpb) == 0)
    o_ref[...] = (b_gate * conv).astype(o_ref.dtype)


def _sc_in(h, w_in, w_conv, seq):
    n, d = h.shape
    tm = min(MM_TILE, seq)
    tc = min(COL_TILE, d)
    nj = d // tc
    k_taps = w_conv.shape[0]
    halo = SUBLANES
    return pl.pallas_call(
        functools.partial(_sc_in_kernel, tpb=seq // tm, halo=halo),
        out_shape=jax.ShapeDtypeStruct((n, d), BF16),
        grid=(nj, n // tm),
        in_specs=[pl.BlockSpec((tm, d), lambda j, i: (i, 0)),
                  pl.BlockSpec((d, tc), lambda j, i: (0, j)),
                  pl.BlockSpec((d, tc), lambda j, i: (0, nj + j)),
                  pl.BlockSpec((d, tc), lambda j, i: (0, 2 * nj + j)),
                  pl.BlockSpec((k_taps, tc), lambda j, i: (0, j))],
        out_specs=pl.BlockSpec((tm, tc), lambda j, i: (i, j)),
        scratch_shapes=[pltpu.VMEM((halo + tm, tc), F32)],
        compiler_params=_params("arbitrary", "arbitrary"),
        name="short_conv_in",
    )(h, w_in, w_in, w_in, w_conv)


def _conf_in_kernel(h_ref, wa_ref, wg_ref, ba_ref, bg_ref, dw_ref, bdw_ref, o_ref, buf, *, tpb, halo):
    i = pl.program_id(1)
    h = h_ref[...]
    a = _dot(h, wa_ref[...]) + ba_ref[...]
    g = _dot(h, wg_ref[...]) + bg_ref[...]
    u = a * jax.nn.sigmoid(g)
    o_ref[...] = _causal_taps(u, buf, dw_ref[...], halo, (i % tpb) == 0) + bdw_ref[...]


def _conf_in(h, w_pw1, b_pw1, w_dw, b_dw, seq):
    n, d = h.shape
    tm = min(MM_TILE, seq)
    tc = min(COL_TILE, d)
    nj = d // tc
    k_taps = w_dw.shape[0]
    halo = -(-(k_taps - 1) // SUBLANES) * SUBLANES
    b1 = b_pw1.reshape(1, 2 * d)
    return pl.pallas_call(
        functools.partial(_conf_in_kernel, tpb=seq // tm, halo=halo),
        out_shape=jax.ShapeDtypeStruct((n, d), F32),
        grid=(nj, n // tm),
        in_specs=[pl.BlockSpec((tm, d), lambda j, i: (i, 0)),
                  pl.BlockSpec((d, tc), lambda j, i: (0, j)),
                  pl.BlockSpec((d, tc), lambda j, i: (0, nj + j)),
                  pl.BlockSpec((1, tc), lambda j, i: (0, j)),
                  pl.BlockSpec((1, tc), lambda j, i: (0, nj + j)),
                  pl.BlockSpec((k_taps, tc), lambda j, i: (0, j)),
                  pl.BlockSpec((1, tc), lambda j, i: (0, j))],
        out_specs=pl.BlockSpec((tm, tc), lambda j, i: (i, j)),
        scratch_shapes=[pltpu.VMEM((halo + tm, tc), F32)],
        compiler_params=_params("arbitrary", "arbitrary"),
        name="conformer_in",
    )(h, w_pw1, w_pw1, b1, b1, w_dw, b_dw.reshape(1, d))


def _proj_res_kernel(*refs, has_bias, has_ln):
    a_ref, w_ref, x_ref, mod_ref = refs[:4]
    rest = list(refs[4:])
    b_ref = rest.pop(0) if has_bias else None
    lng_ref, lnb_ref = (rest.pop(0), rest.pop(0)) if has_ln else (None, None)
    o_ref = rest.pop(0)
    a = a_ref[...]
    if has_ln:
        mu = jnp.mean(a, axis=-1, keepdims=True)
        ac = a - mu
        var = jnp.mean(ac * ac, axis=-1, keepdims=True)
        a = ac * lax.rsqrt(var + EPS) * lng_ref[...] + lnb_ref[...]
        a = (a * jax.nn.sigmoid(a)).astype(BF16)
    y = _dot(a, w_ref[...])
    if has_bias:
        y = y + b_ref[...]
    o_ref[...] = x_ref[...] + mod_ref[0][2:3] * y


def _proj_res(a, w, x2, mod, seq, bias=None, ln=None):
    n, d = x2.shape
    k = a.shape[1]
    tm = min(ROW_TILE, seq)
    tpb = seq // tm
    row = lambda i: (i, 0)
    fixed = lambda i: (0, 0)
    args = [a, w, x2, mod]
    specs = [pl.BlockSpec((tm, k), row), pl.BlockSpec((k, d), fixed), pl.BlockSpec((tm, d), row),
             pl.BlockSpec((1, N_MOD, d), lambda i: (i // tpb, 0, 0))]
    if bias is not None:
        args.append(bias.reshape(1, d))
        specs.append(pl.BlockSpec((1, d), fixed))
    if ln is not None:
        args += [ln[0].reshape(1, k), ln[1].reshape(1, k)]
        specs += [pl.BlockSpec((1, k), fixed), pl.BlockSpec((1, k), fixed)]
    return pl.pallas_call(
        functools.partial(_proj_res_kernel, has_bias=bias is not None, has_ln=ln is not None),
        out_shape=jax.ShapeDtypeStruct((n, d), F32),
        grid=(n // tm,),
        in_specs=specs,
        out_specs=pl.BlockSpec((tm, d), row),
        compiler_params=_params("arbitrary"),
        name="proj_residual",
    )(*args)


def _pool_kernel(x_ref, halo_ref, gain_ref, mod_ref, w_ref, ps_ref, o_ref, buf, *, tpb):
    i = pl.program_id(0)
    tm = x_ref.shape[0]
    grp = w_ref.shape[1]
    m = mod_ref[0]
    gain = gain_ref[...]
    x = x_ref[...]
    h = _rms_mod(x, gain, m[1:2], m[0:1])
    hh = _rms_mod(halo_ref[...], gain, m[1:2], m[0:1])
    tile_in_seq = i % tpb
    buf[0:POOL_HALO] = jnp.where(tile_in_seq == 0, 0.0, hh)
    buf[POOL_HALO:POOL_HALO + tm] = h
    pos = tile_in_seq * tm + lax.broadcasted_iota(jnp.int32, (tm, 1), 0)
    for g, win in enumerate(POOL_WINDOWS):
        cs = slice(g * grp, (g + 1) * grp)
        hg = h[:, cs]
        acc = hg
        for j in range(1, win):
            acc = acc + buf[POOL_HALO - j:POOL_HALO - j + tm, cs]
        cnt = jnp.minimum(pos + 1, win).astype(F32)
        pooled = (acc / cnt - hg).astype(BF16)
        y = _dot(pooled, w_ref[g]) * ps_ref[:, cs]
        o_ref[:, cs] = x[:, cs] + m[2:3, cs] * y


def _pool_mixer(x2, gain, mod, pool_w, pool_scale, seq):
    n, d = x2.shape
    tm = min(ROW_TILE, seq)
    tpb = seq // tm
    hb = tm // POOL_HALO
    ng, grp = pool_w.shape[0], pool_w.shape[1]
    return pl.pallas_call(
        functools.partial(_pool_kernel, tpb=tpb),
        out_shape=jax.ShapeDtypeStruct((n, d), F32),
        grid=(n // tm,),
        in_specs=[pl.BlockSpec((tm, d), lambda i: (i, 0)),
                  pl.BlockSpec((POOL_HALO, d), lambda i: (jnp.maximum(i * hb - 1, 0), 0)),
                  pl.BlockSpec((1, d), lambda i: (0, 0)),
                  pl.BlockSpec((1, N_MOD, d), lambda i: (i // tpb, 0, 0)),
                  pl.BlockSpec((ng, grp, grp), lambda i: (0, 0, 0)),
                  pl.BlockSpec((1, d), lambda i: (0, 0))],
        out_specs=pl.BlockSpec((tm, d), lambda i: (i, 0)),
        scratch_shapes=[pltpu.VMEM((POOL_HALO + tm, d), F32)],
        compiler_params=_params("arbitrary"),
        name="pool_mixer",
    )(x2, x2, gain.reshape(1, d), mod, pool_w, pool_scale.reshape(1, d))


def _qkv_kernel(h_ref, w_ref, gains_ref, o_ref, *, nper):
    j = pl.program_id(1)
    y = _dot(h_ref[...], w_ref[...])
    part = j // nper
    heads = y.shape[1] // HEAD_DIM

    def normed(gain, extra):
        outs = []
        for hh in range(heads):
            yh = y[:, hh * HEAD_DIM:(hh + 1) * HEAD_DIM]
            ms = jnp.mean(yh * yh, axis=-1, keepdims=True)
            outs.append(yh * lax.rsqrt(ms + EPS) * gain * extra)
        return jnp.concatenate(outs, axis=1).astype(o_ref.dtype)

    @pl.when(part == 0)
    def _():
        o_ref[...] = normed(gains_ref[0:1, :], HEAD_DIM ** -0.5)

    @pl.when(part == 1)
    def _():
        o_ref[...] = normed(gains_ref[1:2, :], 1.0)

    @pl.when(part == 2)
    def _():
        o_ref[...] = y.astype(o_ref.dtype)


def _qkv(h, w_qkv, q_gain, k_gain, seq):
    n, d = h.shape
    tm = min(MM_TILE, seq)
    tn = min(COL_TILE, d)
    gains = jnp.zeros((SUBLANES, HEAD_DIM), F32).at[0].set(q_gain).at[1].set(k_gain)
    return pl.pallas_call(
        functools.partial(_qkv_kernel, nper=d // tn),
        out_shape=jax.ShapeDtypeStruct((n, 3 * d), BF16),
        grid=(n // tm, 3 * d // tn),
        in_specs=[pl.BlockSpec((tm, d), lambda i, j: (i, 0)),
                  pl.BlockSpec((d, tn), lambda i, j: (0, j)),
                  pl.BlockSpec((SUBLANES, HEAD_DIM), lambda i, j: (0, 0))],
        out_specs=pl.BlockSpec((tm, tn), lambda i, j: (i, j)),
        compiler_params=_params("arbitrary", "arbitrary"),
        name="qkv_proj",
    )(h, w_qkv, gains)


def _attn_kernel(q_ref, k_ref, v_ref, o_ref):
    qi = pl.program_id(2)
    tq = q_ref.shape[0]
    tk = tq
    q = q_ref[...]
    row = lax.broadcasted_iota(jnp.int32, (tq, tk), 0)
    col = lax.broadcasted_iota(jnp.int32, (tq, tk), 1)
    suffix = (row >= col).astype(BF16)
    valid = col < row

    def scores(kb, masked):
        start = pl.multiple_of(kb * tk, tk)
        k = k_ref[pl.ds(start, tk), :]
        v = v_ref[pl.ds(start, tk), :]
        z = lax.dot_general(q, k, (((1,), (1,)), ((), ())), preferred_element_type=F32)
        sp = jnp.maximum(z, 0.0) + jnp.log(1.0 + jnp.exp(-jnp.abs(z)))
        if masked:
            sp = jnp.where(valid, sp, 0.0)
        hi = sp.astype(BF16)
        lo = (sp - hi.astype(F32)).astype(BF16)
        csum = _dot(hi, suffix) + _dot(lo, suffix)
        return z, csum, v

    z, csum, v = scores(qi, True)
    a = jnp.where(valid, jnp.exp(z - csum), 0.0)
    acc = _dot(a.astype(BF16), v)
    later = csum[:, 0:1]

    def body(it, carry):
        acc, later = carry
        z, csum, v = scores(qi - 1 - it, False)
        a = jnp.exp(z - (csum + later))
        return acc + _dot(a.astype(BF16), v), later + csum[:, 0:1]

    acc, later = lax.fori_loop(0, qi, body, (acc, later))
    o_ref[...] = acc.astype(o_ref.dtype)


def _attention(qkv, bsz, seq):
    n, d3 = qkv.shape
    d = d3 // 3
    heads = d // HEAD_DIM
    tq = min(ATTN_TILE, seq)
    nq = seq // tq
    return pl.pallas_call(
        _attn_kernel,
        out_shape=jax.ShapeDtypeStruct((n, d), BF16),
        grid=(bsz, heads, nq),
        in_specs=[pl.BlockSpec((tq, HEAD_DIM), lambda b, h, i: (b * nq + i, h)),
                  pl.BlockSpec((seq, HEAD_DIM), lambda b, h, i: (b, heads + h)),
                  pl.BlockSpec((seq, HEAD_DIM), lambda b, h, i: (b, 2 * heads + h))],
        out_specs=pl.BlockSpec((tq, HEAD_DIM), lambda b, h, i: (b * nq + i, h)),
        compiler_params=_params("arbitrary", "arbitrary", "arbitrary"),
        name="stick_breaking_attn",
    )(qkv, qkv, qkv)


def _moe_pre_kernel(x_ref, gain_ref, mod_ref, rw_ref, rb_ref,
                    words_ref, idx_ref, gate_ref, rank_ref, cnt_ref, run):
    i = pl.program_id(0)
    tm = x_ref.shape[0]

    @pl.when(i == 0)
    def _():
        run[...] = jnp.zeros_like(run)

    m = mod_ref[0]
    h = _rms_mod(x_ref[...], gain_ref[...], m[4:5], m[3:4])
    _pack_rows(h, words_ref, tm)

    logits = jnp.dot(h, rw_ref[...], preferred_element_type=F32,
                     precision=lax.Precision.HIGHEST) + rb_ref[...]
    lane = lax.broadcasted_iota(jnp.int32, (tm, LANES), 1)
    onehots, vals, idxs = [], [], []
    cur = logits
    for _ in range(TOP_K):
        mx = jnp.max(cur, axis=-1, keepdims=True)
        ix = jnp.min(jnp.where(cur == mx, lane, LANES), axis=-1, keepdims=True)
        oh = lane == ix
        onehots.append(oh)
        vals.append(mx)
        idxs.append(ix)
        cur = jnp.where(oh, -jnp.inf, cur)
    exps = [jnp.exp(v - vals[0]) for v in vals]
    den = exps[0]
    for e in exps[1:]:
        den = den + e

    ohs = jnp.concatenate([oh.astype(BF16) for oh in onehots], axis=1)
    r_i = lax.broadcasted_iota(jnp.int32, (tm, tm), 0)
    c_i = lax.broadcasted_iota(jnp.int32, (tm, tm), 1)
    earlier = _dot((c_i < r_i).astype(BF16), ohs)
    base = run[...]
    idx_out = jnp.zeros((tm, LANES), jnp.int32)
    rank_out = jnp.zeros((tm, LANES), jnp.int32)
    gate_out = jnp.zeros((tm, LANES), F32)
    for k in range(TOP_K):
        ohf = onehots[k].astype(F32)
        rk = jnp.sum(ohf * (earlier[:, k * LANES:(k + 1) * LANES] + base), axis=-1, keepdims=True)
        base = base + jnp.sum(ohf, axis=0, keepdims=True)
        idx_out = jnp.where(lane == k, idxs[k], idx_out)
        rank_out = jnp.where(lane == k, rk.astype(jnp.int32), rank_out)
        gate_out = jnp.where(lane == k, exps[k] / den, gate_out)
    run[...] = base
    cnt_ref[...] = base
    idx_ref[...] = idx_out
    rank_ref[...] = rank_out
    gate_ref[...] = gate_out


def _moe_pre(x1, gain, mod, router_w, router_b, seq):
    n, d = x1.shape
    n_exp = router_w.shape[1]
    tm = min(ROW_TILE, seq)
    tpb = seq // tm
    wr = d // (2 * LANES)
    rw = jnp.zeros((d, LANES), F32).at[:, :n_exp].set(router_w)
    rb = jnp.full((1, LANES), NEG_BIG, F32).at[0, :n_exp].set(router_b)
    row = lambda i: (i, 0)
    fixed = lambda i: (0, 0)
    return pl.pallas_call(
        _moe_pre_kernel,
        out_shape=(jax.ShapeDtypeStruct((n * wr, LANES), jnp.uint32),
                   jax.ShapeDtypeStruct((n, LANES), jnp.int32),
                   jax.ShapeDtypeStruct((n, LANES), F32),
                   jax.ShapeDtypeStruct((n, LANES), jnp.int32),
                   jax.ShapeDtypeStruct((1, LANES), F32)),
        grid=(n // tm,),
        in_specs=[pl.BlockSpec((tm, d), row),
                  pl.BlockSpec((1, d), fixed),
                  pl.BlockSpec((1, N_MOD, d), lambda i: (i // tpb, 0, 0)),
                  pl.BlockSpec((d, LANES), fixed),
                  pl.BlockSpec((1, LANES), fixed)],
        out_specs=(pl.BlockSpec((tm * wr, LANES), row),
                   pl.BlockSpec((tm, LANES), row),
                   pl.BlockSpec((tm, LANES), row),
                   pl.BlockSpec((tm, LANES), row),
                   pl.BlockSpec((1, LANES), fixed)),
        scratch_shapes=[pltpu.VMEM((1, LANES), F32)],
        compiler_params=_params("arbitrary"),
        name="moe_route",
    )(x1, gain.reshape(1, d), mod, rw, rb)


def _row_copy(src, src_row, dst, dst_row, wr, sem):
    return pltpu.make_async_copy(src.at[pl.ds(pl.multiple_of(src_row * wr, wr), wr), :],
                                 dst.at[pl.ds(pl.multiple_of(dst_row * wr, wr), wr), :], sem)


def _scatter_kernel(fill_lo_ref, fill_hi_ref, dest_ref, src, dst, zbuf, sem, zsem, *, wr):
    i = pl.program_id(0)
    n = dest_ref.shape[-1]

    @pl.when(i == 0)
    def _():
        zbuf[...] = jnp.zeros_like(zbuf)

        def per_expert(e, count):
            def per_row(r, c):
                pltpu.make_async_copy(zbuf, dst.at[pl.ds(pl.multiple_of(r * wr, wr), wr), :], zsem).start()
                return c + 1
            return lax.fori_loop(fill_lo_ref[e], fill_hi_ref[e], per_row, count)

        total = lax.fori_loop(0, fill_lo_ref.shape[0], per_expert, 0)

        def drain(r, c):
            pltpu.make_async_copy(zbuf, dst.at[pl.ds(0, wr), :], zsem).wait()
            return c
        lax.fori_loop(0, total, drain, 0)

    def issue(a, c):
        token = lax.shift_right_logical(i * n + a, TOP_K.bit_length() - 1)
        _row_copy(src, token, dst, dest_ref[0, 0, a], wr, sem).start()
        return c
    lax.fori_loop(0, n, issue, 0)

    def drain(a, c):
        _row_copy(src, 0, dst, 0, wr, sem).wait()
        return c
    lax.fori_loop(0, n, drain, 0)


def _gather_kernel(dest_ref, src, dst, sem, *, wr):
    i = pl.program_id(0)
    n = dest_ref.shape[-1]

    def issue(a, c):
        _row_copy(src, dest_ref[0, 0, a], dst, i * n + a, wr, sem).start()
        return c
    lax.fori_loop(0, n, issue, 0)

    def drain(a, c):
        _row_copy(src, 0, dst, 0, wr, sem).wait()
        return c
    lax.fori_loop(0, n, drain, 0)


def _scatter_rows(words, dest3, fill_lo, fill_hi, n_rows, wr):
    steps, _, per = dest3.shape
    return pl.pallas_call(
        functools.partial(_scatter_kernel, wr=wr),
        out_shape=jax.ShapeDtypeStruct((n_rows * wr, LANES), jnp.uint32),
        grid_spec=pltpu.PrefetchScalarGridSpec(
            num_scalar_prefetch=2,
            grid=(steps,),
            in_specs=[pl.BlockSpec((1, 1, per), lambda i, lo, hi: (i, 0, 0), memory_space=pltpu.SMEM),
                      pl.BlockSpec(memory_space=pl.ANY)],
            out_specs=pl.BlockSpec(memory_space=pl.ANY),
            scratch_shapes=[pltpu.VMEM((wr, LANES), jnp.uint32),
                            pltpu.SemaphoreType.DMA(()), pltpu.SemaphoreType.DMA(())]),
        compiler_params=pltpu.CompilerParams(dimension_semantics=("arbitrary",), has_side_effects=True),
        name="moe_scatter_rows",
    )(fill_lo, fill_hi, dest3, words)


def _gather_rows(words, dest3, wr):
    steps, _, per = dest3.shape
    return pl.pallas_call(
        functools.partial(_gather_kernel, wr=wr),
        out_shape=jax.ShapeDtypeStruct((steps * per * wr, LANES), jnp.uint32),
        grid=(steps,),
        in_specs=[pl.BlockSpec((1, 1, per), lambda i: (i, 0, 0), memory_space=pltpu.SMEM),
                  pl.BlockSpec(memory_space=pl.ANY)],
        out_specs=pl.BlockSpec(memory_space=pl.ANY),
        scratch_shapes=[pltpu.SemaphoreType.DMA(())],
        compiler_params=pltpu.CompilerParams(dimension_semantics=("arbitrary",), has_side_effects=True),
        name="moe_gather_rows",
    )(dest3, words)


def _expert_gu_kernel(te_ref, first_ref, nreal_ref, rows_ref, wg_ref, wu_ref, bg_ref, bu_ref,
                      act_ref, wg_s, wu_s, *, wr):
    t = pl.program_id(1)
    tm = act_ref.shape[0]
    half = wr * LANES

    @pl.when(t < nreal_ref[0])
    def _():
        @pl.when(first_ref[t] == 1)
        def _():
            wg_s[...] = wg_ref[0].astype(BF16)
            wu_s[...] = wu_ref[0].astype(BF16)

        his, los = [], []
        for c in range(wr):
            hi, lo = _unpack_slab(rows_ref[pl.ds(c, tm, stride=wr), :])
            his.append(hi.astype(BF16))
            los.append(lo.astype(BF16))
        xa = jnp.concatenate(his, axis=1)
        xb = jnp.concatenate(los, axis=1)
        g = _dot(xa, wg_s[0:half]) + _dot(xb, wg_s[half:2 * half]) + bg_ref[0]
        u = _dot(xa, wu_s[0:half]) + _dot(xb, wu_s[half:2 * half]) + bu_ref[0]
        g = jnp.minimum(g, SWIGLU_LIMIT)
        u = jnp.clip(u, -SWIGLU_LIMIT, SWIGLU_LIMIT)
        act_ref[...] = (g * jax.nn.sigmoid(SWIGLU_ALPHA * g) * (u + 1.0)).astype(act_ref.dtype)

    @pl.when(t >= nreal_ref[0])
    def _():
        act_ref[...] = jnp.zeros_like(act_ref)


def _expert_gu(rows, w_gu, b_gu, tile_e, is_first, n_real, wr):
    n_exp, d, de2 = w_gu.shape
    de = de2 // 2
    tm = EXPERT_TILE
    n_tiles = rows.shape[0] // (tm * wr)
    tn = min(COL_TILE, de)
    nj = de // tn
    b3 = b_gu.reshape(n_exp, 1, de2)

    def tile(t, nr):
        return jnp.minimum(t, nr[0] - 1)

    return pl.pallas_call(
        functools.partial(_expert_gu_kernel, wr=wr),
        out_shape=jax.ShapeDtypeStruct((n_tiles * tm, de), BF16),
        grid_spec=pltpu.PrefetchScalarGridSpec(
            num_scalar_prefetch=3,
            grid=(nj, n_tiles),
            in_specs=[pl.BlockSpec((tm * wr, LANES), lambda j, t, te, fi, nr: (tile(t, nr), 0)),
                      pl.BlockSpec((1, d, tn), lambda j, t, te, fi, nr: (te[tile(t, nr)], 0, j)),
                      pl.BlockSpec((1, d, tn), lambda j, t, te, fi, nr: (te[tile(t, nr)], 0, nj + j)),
                      pl.BlockSpec((1, 1, tn), lambda j, t, te, fi, nr: (te[tile(t, nr)], 0, j)),
                      pl.BlockSpec((1, 1, tn), lambda j, t, te, fi, nr: (te[tile(t, nr)], 0, nj + j))],
            out_specs=pl.BlockSpec((tm, tn), lambda j, t, te, fi, nr: (t, j)),
            scratch_shapes=[pltpu.VMEM((d, tn), BF16), pltpu.VMEM((d, tn), BF16)]),
        compiler_params=_params("arbitrary", "arbitrary"),
        name="expert_gate_up",
    )(tile_e, is_first, n_real, rows, w_gu, w_gu, b3, b3)


def _expert_down_kernel(te_ref, first_ref, nreal_ref, act_ref, wd_ref, bd_ref, out_ref, wd_s):
    t = pl.program_id(0)
    tm = act_ref.shape[0]

    @pl.when(t < nreal_ref[0])
    def _():
        @pl.when(first_ref[t] == 1)
        def _():
            wd_s[...] = wd_ref[0].astype(BF16)

        _pack_rows(_dot(act_ref[...], wd_s[...]) + bd_ref[0], out_ref, tm)

    @pl.when(t >= nreal_ref[0])
    def _():
        out_ref[...] = jnp.zeros_like(out_ref)


def _expert_down(act, w_down, b_down, tile_e, is_first, n_real, wr):
    n_exp, de, d = w_down.shape
    tm = EXPERT_TILE
    n_tiles = act.shape[0] // tm

    def tile(t, nr):
        return jnp.minimum(t, nr[0] - 1)

    return pl.pallas_call(
        _expert_down_kernel,
        out_shape=jax.ShapeDtypeStruct((n_tiles * tm * wr, LANES), jnp.uint32),
        grid_spec=pltpu.PrefetchScalarGridSpec(
            num_scalar_prefetch=3,
            grid=(n_tiles,),
            in_specs=[pl.BlockSpec((tm, de), lambda t, te, fi, nr: (tile(t, nr), 0)),
                      pl.BlockSpec((1, de, d), lambda t, te, fi, nr: (te[tile(t, nr)], 0, 0)),
                      pl.BlockSpec((1, 1, d), lambda t, te, fi, nr: (te[tile(t, nr)], 0, 0))],
            out_specs=pl.BlockSpec((tm * wr, LANES), lambda t, te, fi, nr: (t, 0)),
            scratch_shapes=[pltpu.VMEM((de, d), BF16)]),
        compiler_params=_params("arbitrary"),
        name="expert_down",
    )(tile_e, is_first, n_real, act, w_down, b_down.reshape(n_exp, 1, d))


def _combine_kernel(*refs, wr, emit_h):
    y4_ref, gate_ref, x_ref, mod_ref = refs[:4]
    if emit_h:
        gain_ref, modn_ref, xo_ref, ho_ref = refs[4:]
    else:
        xo_ref, = refs[4:]
    tm = x_ref.shape[0]
    half = wr * LANES
    m = mod_ref[0]
    gate = gate_ref[...]
    gates = [gate[:, k:k + 1] for k in range(TOP_K)]
    for c in range(wr):
        acc_hi = jnp.zeros((tm, LANES), F32)
        acc_lo = jnp.zeros((tm, LANES), F32)
        for k in range(TOP_K):
            hi, lo = _unpack_slab(y4_ref[pl.ds(k * wr + c, tm, stride=TOP_K * wr), :])
            acc_hi = acc_hi + gates[k] * hi
            acc_lo = acc_lo + gates[k] * lo
        c0 = slice(c * LANES, (c + 1) * LANES)
        c1 = slice(half + c * LANES, half + (c + 1) * LANES)
        xo_ref[:, c0] = x_ref[:, c0] + m[5:6, c0] * acc_hi
        xo_ref[:, c1] = x_ref[:, c1] + m[5:6, c1] * acc_lo
    if emit_h:
        mn = modn_ref[0]
        ho_ref[...] = _rms_mod(xo_ref[...], gain_ref[...], mn[1:2], mn[0:1]).astype(ho_ref.dtype)


def _combine(y4, gate, x1, mod, seq, wr, next_gain=None, next_mod=None):
    n, d = x1.shape
    tm = min(ROW_TILE, seq)
    tpb = seq // tm
    emit_h = next_gain is not None
    row = lambda i: (i, 0)
    batch = lambda i: (i // tpb, 0, 0)
    args = [y4, gate, x1, mod]
    specs = [pl.BlockSpec((tm * TOP_K * wr, LANES), row), pl.BlockSpec((tm, LANES), row),
             pl.BlockSpec((tm, d), row), pl.BlockSpec((1, N_MOD, d), batch)]
    out_shape = [jax.ShapeDtypeStruct((n, d), F32)]
    out_specs = [pl.BlockSpec((tm, d), row)]
    if emit_h:
        args += [next_gain.reshape(1, d), next_mod]
        specs += [pl.BlockSpec((1, d), lambda i: (0, 0)), pl.BlockSpec((1, N_MOD, d), batch)]
        out_shape.append(jax.ShapeDtypeStruct((n, d), BF16))
        out_specs.append(pl.BlockSpec((tm, d), row))
    outs = pl.pallas_call(
        functools.partial(_combine_kernel, wr=wr, emit_h=emit_h),
        out_shape=tuple(out_shape),
        grid=(n // tm,),
        in_specs=specs,
        out_specs=tuple(out_specs),
        compiler_params=_params("arbitrary"),
        name="moe_combine",
    )(*args)
    return (outs[0], outs[1]) if emit_h else (outs[0], None)


def _routed_experts(x1, gain, mod, router_w, router_b, w_gu, b_gu, w_down, b_down, seq,
                    next_gain, next_mod):
    n, d = x1.shape
    n_exp = router_w.shape[1]
    wr = d // (2 * LANES)
    words, idx, gate, rank, counts = _moe_pre(x1, gain, mod, router_w, router_b, seq)

    tm = EXPERT_TILE
    n_tiles = (n * TOP_K) // tm + n_exp
    cnt = counts[0, :n_exp].astype(jnp.int32)
    padded = (cnt + tm - 1) // tm * tm
    pad_end = jnp.cumsum(padded)
    pad_start = pad_end - padded
    dest = pad_start[idx[:, :TOP_K]] + rank[:, :TOP_K]
    per = min(PERMUTE_TOKENS, n) * TOP_K
    dest3 = dest.reshape(n * TOP_K // per, 1, per)
    tile_start = jnp.arange(n_tiles, dtype=jnp.int32) * tm
    n_real = (pad_end[-1] // tm).astype(jnp.int32).reshape(1)
    tile_e = jnp.minimum(jnp.searchsorted(pad_end, tile_start, side="right"), n_exp - 1).astype(jnp.int32)
    is_first = ((tile_start == pad_start[tile_e]) & (tile_start < pad_end[-1])).astype(jnp.int32)

    fill_lo = jnp.concatenate([pad_start + cnt, pad_end[-1:]])
    fill_hi = jnp.concatenate([pad_end, jnp.full((1,), n_tiles * tm, jnp.int32)])
    rows = _scatter_rows(words, dest3, fill_lo, fill_hi, n_tiles * tm, wr)
    act = _expert_gu(rows, w_gu, b_gu, tile_e, is_first, n_real, wr)
    out_words = _expert_down(act, w_down, b_down, tile_e, is_first, n_real, wr)
    y4 = _gather_rows(out_words, dest3, wr)
    return _combine(y4, gate, x1, mod, seq, wr, next_gain, next_mod)


def kernel(x, c, w_ada, b_ada, ada_table, norm_mix, norm_ffn, sc_w_in, sc_w_conv, sc_w_out, pool_w, pool_scale, cf_w_pw1, cf_b_pw1, cf_w_dw, cf_b_dw, cf_ln_g, cf_ln_b, cf_w_pw2, cf_b_pw2, sb_w_qkv, sb_q_gain, sb_k_gain, sb_w_o, router_w, router_b, exp_w_gu, exp_b_gu, exp_w_down, exp_b_down):
    bsz, seq, d = x.shape
    depth = norm_mix.shape[0]
    n_mixers = len(POOL_WINDOWS)
    x2 = x.reshape(bsz * seq, d)
    mod_all = _ada(c, w_ada, b_ada, ada_table)
    h = None
    for i in range(depth):
        kind, occ = i % n_mixers, i // n_mixers
        mod = mod_all[i]
        if kind != 1 and h is None:
            h = _norm_mod(x2, norm_mix[i], mod, seq)
        if kind == 0:
            g = _sc_in(h, sc_w_in[occ].astype(BF16), sc_w_conv[occ], seq)
            x1 = _proj_res(g, sc_w_out[occ].astype(BF16), x2, mod, seq)
        elif kind == 1:
            x1 = _pool_mixer(x2, norm_mix[i], mod, pool_w[occ].astype(BF16), pool_scale[occ], seq)
        elif kind == 2:
            u = _conf_in(h, cf_w_pw1[occ].astype(BF16), cf_b_pw1[occ], cf_w_dw[occ], cf_b_dw[occ], seq)
            x1 = _proj_res(u, cf_w_pw2[occ].astype(BF16), x2, mod, seq, bias=cf_b_pw2[occ],
                           ln=(cf_ln_g[occ], cf_ln_b[occ]))
        else:
            qkv = _qkv(h, sb_w_qkv[occ].astype(BF16), sb_q_gain[occ], sb_k_gain[occ], seq)
            o = _attention(qkv, bsz, seq)
            x1 = _proj_res(o, sb_w_o[occ].astype(BF16), x2, mod, seq)
        nxt = i + 1
        wants_h = nxt < depth and nxt % n_mixers != 1
        x2, h = _routed_experts(
            x1, norm_ffn[i], mod, router_w[i], router_b[i], exp_w_gu[i], exp_b_gu[i],
            exp_w_down[i], exp_b_down[i], seq,
            norm_mix[nxt] if wants_h else None, mod_all[nxt] if wants_h else None)
    return x2.reshape(bsz, seq, d)
```

```python
import functools

import jax
import jax.numpy as jnp
from jax import lax
from jax.experimental import pallas as pl
from jax.experimental.pallas import tpu as pltpu

EPS = 1e-6
LANES = 128
SUBLANES = 8
HEAD_DIM = 128
POOL_WINDOWS = (2, 4, 8, 16)
POOL_HALO = 16
TOP_K = 4
SWIGLU_LIMIT = 7.0
SWIGLU_ALPHA = 1.702
N_MOD = 6
VMEM_LIMIT_BYTES = 52 * 1024 * 1024
NEG_BIG = -1e30
ROW_TILE = 256
MM_TILE = 512
COL_TILE = 512
EXPERT_TILE = 256
ATTN_TILE = 256
ATTN_KEYS = 256
ATTN_HEADS = 4
PERMUTE_TOKENS = 256

F32 = jnp.float32
BF16 = jnp.bfloat16


def _params(*sem):
    return pltpu.CompilerParams(dimension_semantics=sem, vmem_limit_bytes=VMEM_LIMIT_BYTES)


def _rms_mod(x, gain, scale, shift):
    ms = jnp.mean(x * x, axis=-1, keepdims=True)
    return x * lax.rsqrt(ms + EPS) * gain * (1.0 + scale) + shift


def _dot(a, b):
    return jnp.dot(a, b, preferred_element_type=F32)


def _pack_rows(val, ref, n_rows):
    half = val.shape[1] // 2
    wr = half // LANES
    bits = pltpu.bitcast(val.astype(BF16).astype(F32), jnp.uint32)
    word = (bits[:, :half] & jnp.uint32(0xFFFF0000)) | (bits[:, half:] >> 16)
    for c in range(wr):
        ref[pl.ds(c, n_rows, stride=wr), :] = word[:, c * LANES:(c + 1) * LANES]


def _unpack_slab(w):
    hi = pltpu.bitcast(w & jnp.uint32(0xFFFF0000), F32)
    lo = pltpu.bitcast(w << 16, F32)
    return hi, lo


def _ada_kernel(c_ref, w_ref, b_ref, t_ref, o_ref):
    c = c_ref[...]
    a = c * jax.nn.sigmoid(c)
    shared = jnp.dot(a, w_ref[...], preferred_element_type=F32,
                     precision=lax.Precision.HIGHEST) + b_ref[...]
    for i in range(o_ref.shape[0]):
        o_ref[i] = shared + t_ref[i:i + 1, :]


def _ada(c, w_ada, b_ada, ada_table):
    bsz, d = c.shape
    depth = ada_table.shape[0]
    n = w_ada.shape[1]
    c8 = jnp.zeros((SUBLANES, d), F32).at[:bsz].set(c)
    tn = min(COL_TILE, n)
    out = pl.pallas_call(
        _ada_kernel,
        out_shape=jax.ShapeDtypeStruct((depth, SUBLANES, n), F32),
        grid=(n // tn,),
        in_specs=[pl.BlockSpec((SUBLANES, d), lambda j: (0, 0)),
                  pl.BlockSpec((d, tn), lambda j: (0, j)),
                  pl.BlockSpec((1, tn), lambda j: (0, j)),
                  pl.BlockSpec((depth, tn), lambda j: (0, j))],
        out_specs=pl.BlockSpec((depth, SUBLANES, tn), lambda j: (0, 0, j)),
        compiler_params=_params("arbitrary"),
        name="ada_mod",
    )(c8, w_ada, b_ada.reshape(1, n), ada_table.reshape(depth, n))
    return out[:, :bsz].reshape(depth, bsz, N_MOD, d)


def _norm_mod_kernel(x_ref, g_ref, mod_ref, o_ref):
    m = mod_ref[0]
    o_ref[...] = _rms_mod(x_ref[...], g_ref[...], m[1:2], m[0:1]).astype(o_ref.dtype)


def _norm_mod(x2, gain, mod, seq):
    n, d = x2.shape
    tm = min(ROW_TILE, seq)
    tpb = seq // tm
    return pl.pallas_call(
        _norm_mod_kernel,
        out_shape=jax.ShapeDtypeStruct((n, d), BF16),
        grid=(n // tm,),
        in_specs=[pl.BlockSpec((tm, d), lambda i: (i, 0)),
                  pl.BlockSpec((1, d), lambda i: (0, 0)),
                  pl.BlockSpec((1, N_MOD, d), lambda i: (i // tpb, 0, 0))],
        out_specs=pl.BlockSpec((tm, d), lambda i: (i, 0)),
        compiler_params=_params("arbitrary"),
        name="norm_mod",
    )(x2, gain.reshape(1, d), mod)


def _causal_taps(val, buf, taps, halo, first):
    tm = val.shape[0]
    k_taps = taps.shape[0]

    @pl.when(first)
    def _():
        buf[0:halo] = jnp.zeros((halo, val.shape[1]), F32)

    @pl.when(jnp.logical_not(first))
    def _():
        buf[0:halo] = buf[tm:tm + halo]

    buf[halo:halo + tm] = val
    acc = taps[k_taps - 1:k_taps] * val
    for k in range(k_taps - 1):
        off = halo - (k_taps - 1) + k
        acc = acc + taps[k:k + 1] * buf[off:off + tm]
    return acc


def _sc_in_kernel(h_ref, wb_ref, wc_ref, wu_ref, cw_ref, o_ref, buf, *, tpb, halo):
    i = pl.program_id(1)
    h = h_ref[...]
    b_gate = _dot(h, wb_ref[...])
    v = _dot(h, wc_ref[...]) * _dot(h, wu_ref[...])
    conv = _causal_taps(v, buf, cw_ref[...], halo, (i % tpb) == 0)
    o_ref[...] = (b_gate * conv).astype(o_ref.dtype)


def _sc_in(h, w_in, w_conv, seq):
    n, d = h.shape
    tm = min(MM_TILE, seq)
    tc = min(COL_TILE, d)
    nj = d // tc
    k_taps = w_conv.shape[0]
    halo = SUBLANES
    return pl.pallas_call(
        functools.partial(_sc_in_kernel, tpb=seq // tm, halo=halo),
        out_shape=jax.ShapeDtypeStruct((n, d), BF16),
        grid=(nj, n // tm),
        in_specs=[pl.BlockSpec((tm, d), lambda j, i: (i, 0)),
                  pl.BlockSpec((d, tc), lambda j, i: (0, j)),
                  pl.BlockSpec((d, tc), lambda j, i: (0, nj + j)),
                  pl.BlockSpec((d, tc), lambda j, i: (0, 2 * nj + j)),
                  pl.BlockSpec((k_taps, tc), lambda j, i: (0, j))],
        out_specs=pl.BlockSpec((tm, tc), lambda j, i: (i, j)),
        scratch_shapes=[pltpu.VMEM((halo + tm, tc), F32)],
        compiler_params=_params("arbitrary", "arbitrary"),
        name="short_conv_in",
    )(h, w_in, w_in, w_in, w_conv)


def _conf_in_kernel(h_ref, wa_ref, wg_ref, ba_ref, bg_ref, dw_ref, bdw_ref, o_ref, buf, *, tpb, halo):
    i = pl.program_id(1)
    h = h_ref[...]
    a = _dot(h, wa_ref[...]) + ba_ref[...]
    g = _dot(h, wg_ref[...]) + bg_ref[...]
    u = a * jax.nn.sigmoid(g)
    o_ref[...] = _causal_taps(u, buf, dw_ref[...], halo, (i % tpb) == 0) + bdw_ref[...]


def _conf_in(h, w_pw1, b_pw1, w_dw, b_dw, seq):
    n, d = h.shape
    tm = min(MM_TILE, seq)
    tc = min(COL_TILE, d)
    nj = d // tc
    k_taps = w_dw.shape[0]
    halo = -(-(k_taps - 1) // SUBLANES) * SUBLANES
    b1 = b_pw1.reshape(1, 2 * d)
    return pl.pallas_call(
        functools.partial(_conf_in_kernel, tpb=seq // tm, halo=halo),
        out_shape=jax.ShapeDtypeStruct((n, d), F32),
        grid=(nj, n // tm),
        in_specs=[pl.BlockSpec((tm, d), lambda j, i: (i, 0)),
                  pl.BlockSpec((d, tc), lambda j, i: (0, j)),
                  pl.BlockSpec((d, tc), lambda j, i: (0, nj + j)),
                  pl.BlockSpec((1, tc), lambda j, i: (0, j)),
                  pl.BlockSpec((1, tc), lambda j, i: (0, nj + j)),
                  pl.BlockSpec((k_taps, tc), lambda j, i: (0, j)),
                  pl.BlockSpec((1, tc), lambda j, i: (0, j))],
        out_specs=pl.BlockSpec((tm, tc), lambda j, i: (i, j)),
        scratch_shapes=[pltpu.VMEM((halo + tm, tc), F32)],
        compiler_params=_params("arbitrary", "arbitrary"),
        name="conformer_in",
    )(h, w_pw1, w_pw1, b1, b1, w_dw, b_dw.reshape(1, d))


def _proj_res_kernel(*refs, has_bias, has_ln):
    a_ref, w_ref, x_ref, mod_ref = refs[:4]
    rest = list(refs[4:])
    b_ref = rest.pop(0) if has_bias else None
    lng_ref, lnb_ref = (rest.pop(0), rest.pop(0)) if has_ln else (None, None)
    o_ref = rest.pop(0)
    a = a_ref[...]
    if has_ln:
        mu = jnp.mean(a, axis=-1, keepdims=True)
        ac = a - mu
        var = jnp.mean(ac * ac, axis=-1, keepdims=True)
        a = ac * lax.rsqrt(var + EPS) * lng_ref[...] + lnb_ref[...]
        a = (a * jax.nn.sigmoid(a)).astype(BF16)
    y = _dot(a, w_ref[...])
    if has_bias:
        y = y + b_ref[...]
    o_ref[...] = x_ref[...] + mod_ref[0][2:3] * y


def _proj_res(a, w, x2, mod, seq, bias=None, ln=None):
    n, d = x2.shape
    k = a.shape[1]
    tm = min(ROW_TILE, seq)
    tpb = seq // tm
    row = lambda i: (i, 0)
    fixed = lambda i: (0, 0)
    args = [a, w, x2, mod]
    specs = [pl.BlockSpec((tm, k), row), pl.BlockSpec((k, d), fixed), pl.BlockSpec((tm, d), row),
             pl.BlockSpec((1, N_MOD, d), lambda i: (i // tpb, 0, 0))]
    if bias is not None:
        args.append(bias.reshape(1, d))
        specs.append(pl.BlockSpec((1, d), fixed))
    if ln is not None:
        args += [ln[0].reshape(1, k), ln[1].reshape(1, k)]
        specs += [pl.BlockSpec((1, k), fixed), pl.BlockSpec((1, k), fixed)]
    return pl.pallas_call(
        functools.partial(_proj_res_kernel, has_bias=bias is not None, has_ln=ln is not None),
        out_shape=jax.ShapeDtypeStruct((n, d), F32),
        grid=(n // tm,),
        in_specs=specs,
        out_specs=pl.BlockSpec((tm, d), row),
        compiler_params=_params("arbitrary"),
        name="proj_residual",
    )(*args)


def _pool_kernel(x_ref, halo_ref, gain_ref, mod_ref, w_ref, ps_ref, o_ref, buf, *, tpb):
    i = pl.program_id(0)
    tm = x_ref.shape[0]
    grp = w_ref.shape[1]
    m = mod_ref[0]
    gain = gain_ref[...]
    x = x_ref[...]
    h = _rms_mod(x, gain, m[1:2], m[0:1])
    hh = _rms_mod(halo_ref[...], gain, m[1:2], m[0:1])
    tile_in_seq = i % tpb
    buf[0:POOL_HALO] = jnp.where(tile_in_seq == 0, 0.0, hh)
    buf[POOL_HALO:POOL_HALO + tm] = h
    pos = tile_in_seq * tm + lax.broadcasted_iota(jnp.int32, (tm, 1), 0)
    for g, win in enumerate(POOL_WINDOWS):
        cs = slice(g * grp, (g + 1) * grp)
        hg = h[:, cs]
        acc = hg
        for j in range(1, win):
            acc = acc + buf[POOL_HALO - j:POOL_HALO - j + tm, cs]
        cnt = jnp.minimum(pos + 1, win).astype(F32)
        pooled = (acc / cnt - hg).astype(BF16)
        y = _dot(pooled, w_ref[g]) * ps_ref[:, cs]
        o_ref[:, cs] = x[:, cs] + m[2:3, cs] * y


def _pool_mixer(x2, gain, mod, pool_w, pool_scale, seq):
    n, d = x2.shape
    tm = min(ROW_TILE, seq)
    tpb = seq // tm
    hb = tm // POOL_HALO
    ng, grp = pool_w.shape[0], pool_w.shape[1]
    return pl.pallas_call(
        functools.partial(_pool_kernel, tpb=tpb),
        out_shape=jax.ShapeDtypeStruct((n, d), F32),
        grid=(n // tm,),
        in_specs=[pl.BlockSpec((tm, d), lambda i: (i, 0)),
                  pl.BlockSpec((POOL_HALO, d), lambda i: (jnp.maximum(i * hb - 1, 0), 0)),
                  pl.BlockSpec((1, d), lambda i: (0, 0)),
                  pl.BlockSpec((1, N_MOD, d), lambda i: (i // tpb, 0, 0)),
                  pl.BlockSpec((ng, grp, grp), lambda i: (0, 0, 0)),
                  pl.BlockSpec((1, d), lambda i: (0, 0))],
        out_specs=pl.BlockSpec((tm, d), lambda i: (i, 0)),
        scratch_shapes=[pltpu.VMEM((POOL_HALO + tm, d), F32)],
        compiler_params=_params("arbitrary"),
        name="pool_mixer",
    )(x2, x2, gain.reshape(1, d), mod, pool_w, pool_scale.reshape(1, d))


def _qkv_kernel(h_ref, w_ref, gains_ref, o_ref, *, nper):
    j = pl.program_id(1)
    y = _dot(h_ref[...], w_ref[...])
    part = j // nper
    heads = y.shape[1] // HEAD_DIM

    def normed(gain, extra):
        outs = []
        for hh in range(heads):
            yh = y[:, hh * HEAD_DIM:(hh + 1) * HEAD_DIM]
            ms = jnp.mean(yh * yh, axis=-1, keepdims=True)
            outs.append(yh * lax.rsqrt(ms + EPS) * gain * extra)
        return jnp.concatenate(outs, axis=1).astype(o_ref.dtype)

    @pl.when(part == 0)
    def _():
        o_ref[...] = normed(gains_ref[0:1, :], HEAD_DIM ** -0.5)

    @pl.when(part == 1)
    def _():
        o_ref[...] = normed(gains_ref[1:2, :], 1.0)

    @pl.when(part == 2)
    def _():
        o_ref[...] = y.astype(o_ref.dtype)


def _qkv(h, w_qkv, q_gain, k_gain, seq):
    n, d = h.shape
    tm = min(MM_TILE, seq)
    tn = min(COL_TILE, d)
    gains = jnp.zeros((SUBLANES, HEAD_DIM), F32).at[0].set(q_gain).at[1].set(k_gain)
    return pl.pallas_call(
        functools.partial(_qkv_kernel, nper=d // tn),
        out_shape=jax.ShapeDtypeStruct((n, 3 * d), BF16),
        grid=(n // tm, 3 * d // tn),
        in_specs=[pl.BlockSpec((tm, d), lambda i, j: (i, 0)),
                  pl.BlockSpec((d, tn), lambda i, j: (0, j)),
                  pl.BlockSpec((SUBLANES, HEAD_DIM), lambda i, j: (0, 0))],
        out_specs=pl.BlockSpec((tm, tn), lambda i, j: (i, j)),
        compiler_params=_params("arbitrary", "arbitrary"),
        name="qkv_proj",
    )(h, w_qkv, gains)


def _attn_kernel(q_ref, k_ref, v_ref, o_ref):
    qi = pl.program_id(2)
    tq = q_ref.shape[0]
    tk = min(ATTN_KEYS, k_ref.shape[0])
    heads = q_ref.shape[1] // HEAD_DIM
    diag = (qi * tq) // tk
    row = lax.broadcasted_iota(jnp.int32, (2 * tk, tk), 0)
    col = lax.broadcasted_iota(jnp.int32, (2 * tk, tk), 1)
    suffix2 = (jnp.where(row >= tk, row - tk, row) >= col).astype(BF16)
    valid = (diag * tk + lax.broadcasted_iota(jnp.int32, (tq, tk), 1)
             < qi * tq + lax.broadcasted_iota(jnp.int32, (tq, tk), 0))
    qs = [q_ref[:, hh * HEAD_DIM:(hh + 1) * HEAD_DIM] for hh in range(heads)]

    def blocks(kb, laters, masked):
        start = pl.multiple_of(kb * tk, tk)
        hs = range(heads)
        cols = [slice(hh * HEAD_DIM, (hh + 1) * HEAD_DIM) for hh in hs]
        zs = [lax.dot_general(qs[hh], k_ref[pl.ds(start, tk), cols[hh]], (((1,), (1,)), ((), ())),
                              preferred_element_type=F32) for hh in hs]
        csums = []
        for z in zs:
            sp = jnp.maximum(z, 0.0) + jnp.log(1.0 + jnp.exp(-jnp.abs(z)))
            if masked:
                sp = jnp.where(valid, sp, 0.0)
            hi = pltpu.bitcast(pltpu.bitcast(sp, jnp.uint32) & jnp.uint32(0xFFFF0000), F32)
            parts = jnp.concatenate([hi.astype(BF16), (sp - hi).astype(BF16)], axis=1)
            csums.append(_dot(parts, suffix2))
        outs = []
        for hh in hs:
            a = jnp.exp(zs[hh] - (csums[hh] + laters[hh]))
            if masked:
                a = jnp.where(valid, a, 0.0)
            outs.append(_dot(a.astype(BF16), v_ref[pl.ds(start, tk), cols[hh]]))
        return outs, [laters[hh] + csums[hh][:, 0:1] for hh in hs]

    zero = jnp.zeros((tq, 1), F32)
    accs, laters = blocks(diag, [zero] * heads, True)

    def body(it, carry):
        accs, laters = carry
        outs, laters = blocks(diag - 1 - it, laters, False)
        return [acc + out for acc, out in zip(accs, outs)], laters

    accs, laters = lax.fori_loop(0, diag, body, (accs, laters))
    o_ref[...] = jnp.concatenate(accs, axis=1).astype(o_ref.dtype)


def _attention(qkv, bsz, seq):
    n, d3 = qkv.shape
    d = d3 // 3
    hw = ATTN_HEADS * HEAD_DIM
    ng = d // hw
    tq = min(ATTN_TILE, seq)
    nq = seq // tq
    return pl.pallas_call(
        _attn_kernel,
        out_shape=jax.ShapeDtypeStruct((n, d), BF16),
        grid=(bsz, ng, nq),
        in_specs=[pl.BlockSpec((tq, hw), lambda b, h, i: (b * nq + i, h)),
                  pl.BlockSpec((seq, hw), lambda b, h, i: (b, ng + h)),
                  pl.BlockSpec((seq, hw), lambda b, h, i: (b, 2 * ng + h))],
        out_specs=pl.BlockSpec((tq, hw), lambda b, h, i: (b * nq + i, h)),
        compiler_params=_params("arbitrary", "arbitrary", "arbitrary"),
        name="stick_breaking_attn",
    )(qkv, qkv, qkv)


def _moe_pre_kernel(x_ref, gain_ref, mod_ref, rw_ref, rb_ref,
                    words_ref, idx_ref, gate_ref, rank_ref, cnt_ref, run):
    i = pl.program_id(0)
    tm = x_ref.shape[0]

    @pl.when(i == 0)
    def _():
        run[...] = jnp.zeros_like(run)

    m = mod_ref[0]
    h = _rms_mod(x_ref[...], gain_ref[...], m[4:5], m[3:4])
    _pack_rows(h, words_ref, tm)

    logits = jnp.dot(h, rw_ref[...], preferred_element_type=F32,
                     precision=lax.Precision.HIGHEST) + rb_ref[...]
    lane = lax.broadcasted_iota(jnp.int32, (tm, LANES), 1)
    onehots, vals, idxs = [], [], []
    cur = logits
    for _ in range(TOP_K):
        mx = jnp.max(cur, axis=-1, keepdims=True)
        ix = jnp.min(jnp.where(cur == mx, lane, LANES), axis=-1, keepdims=True)
        oh = lane == ix
        onehots.append(oh)
        vals.append(mx)
        idxs.append(ix)
        cur = jnp.where(oh, -jnp.inf, cur)
    exps = [jnp.exp(v - vals[0]) for v in vals]
    den = exps[0]
    for e in exps[1:]:
        den = den + e

    ohs = jnp.concatenate([oh.astype(BF16) for oh in onehots], axis=1)
    r_i = lax.broadcasted_iota(jnp.int32, (tm, tm), 0)
    c_i = lax.broadcasted_iota(jnp.int32, (tm, tm), 1)
    earlier = _dot((c_i < r_i).astype(BF16), ohs)
    base = run[...]
    idx_out = jnp.zeros((tm, LANES), jnp.int32)
    rank_out = jnp.zeros((tm, LANES), jnp.int32)
    gate_out = jnp.zeros((tm, LANES), F32)
    for k in range(TOP_K):
        ohf = onehots[k].astype(F32)
        rk = jnp.sum(ohf * (earlier[:, k * LANES:(k + 1) * LANES] + base), axis=-1, keepdims=True)
        base = base + jnp.sum(ohf, axis=0, keepdims=True)
        idx_out = jnp.where(lane == k, idxs[k], idx_out)
        rank_out = jnp.where(lane == k, rk.astype(jnp.int32), rank_out)
        gate_out = jnp.where(lane == k, exps[k] / den, gate_out)
    run[...] = base
    cnt_ref[...] = base
    idx_ref[...] = idx_out
    rank_ref[...] = rank_out
    gate_ref[...] = gate_out


def _moe_pre(x1, gain, mod, router_w, router_b, seq):
    n, d = x1.shape
    n_exp = router_w.shape[1]
    tm = min(ROW_TILE, seq)
    tpb = seq // tm
    wr = d // (2 * LANES)
    rw = jnp.zeros((d, LANES), F32).at[:, :n_exp].set(router_w)
    rb = jnp.full((1, LANES), NEG_BIG, F32).at[0, :n_exp].set(router_b)
    row = lambda i: (i, 0)
    fixed = lambda i: (0, 0)
    return pl.pallas_call(
        _moe_pre_kernel,
        out_shape=(jax.ShapeDtypeStruct((n * wr, LANES), jnp.uint32),
                   jax.ShapeDtypeStruct((n, LANES), jnp.int32),
                   jax.ShapeDtypeStruct((n, LANES), F32),
                   jax.ShapeDtypeStruct((n, LANES), jnp.int32),
                   jax.ShapeDtypeStruct((1, LANES), F32)),
        grid=(n // tm,),
        in_specs=[pl.BlockSpec((tm, d), row),
                  pl.BlockSpec((1, d), fixed),
                  pl.BlockSpec((1, N_MOD, d), lambda i: (i // tpb, 0, 0)),
                  pl.BlockSpec((d, LANES), fixed),
                  pl.BlockSpec((1, LANES), fixed)],
        out_specs=(pl.BlockSpec((tm * wr, LANES), row),
                   pl.BlockSpec((tm, LANES), row),
                   pl.BlockSpec((tm, LANES), row),
                   pl.BlockSpec((tm, LANES), row),
                   pl.BlockSpec((1, LANES), fixed)),
        scratch_shapes=[pltpu.VMEM((1, LANES), F32)],
        compiler_params=_params("arbitrary"),
        name="moe_route",
    )(x1, gain.reshape(1, d), mod, rw, rb)


def _rows(ref, row, wr):
    return ref.at[pl.ds(pl.multiple_of(row * wr, wr), wr), :]


def _scatter_kernel(fill_lo_ref, fill_hi_ref, dest_ref, src_ref, dst, zbuf, sem, zsem, *, wr):
    i = pl.program_id(0)
    n = dest_ref.shape[-1]

    @pl.when(i == 0)
    def _():
        zbuf[...] = jnp.zeros_like(zbuf)

        def per_range(e, count):
            def per_row(r, c):
                pltpu.make_async_copy(zbuf, _rows(dst, r, wr), zsem).start()
                return c + 1
            return lax.fori_loop(fill_lo_ref[e], fill_hi_ref[e], per_row, count)

        total = lax.fori_loop(0, fill_lo_ref.shape[0], per_range, 0)

        def drain(r, c):
            pltpu.make_async_copy(zbuf, _rows(dst, 0, wr), zsem).wait()
            return c
        lax.fori_loop(0, total, drain, 0)

    def issue(a, c):
        token = lax.shift_right_logical(a, TOP_K.bit_length() - 1)
        pltpu.make_async_copy(_rows(src_ref, token, wr), _rows(dst, dest_ref[0, 0, a], wr), sem).start()
        return c
    lax.fori_loop(0, n, issue, 0, unroll=8)

    def drain(a, c):
        pltpu.make_async_copy(_rows(src_ref, 0, wr), _rows(dst, 0, wr), sem).wait()
        return c
    lax.fori_loop(0, n, drain, 0, unroll=8)


def _gather_kernel(dest_ref, src, out_ref, sem, *, wr):
    n = dest_ref.shape[-1]

    def issue(a, c):
        pltpu.make_async_copy(_rows(src, dest_ref[0, 0, a], wr), _rows(out_ref, a, wr), sem).start()
        return c
    lax.fori_loop(0, n, issue, 0, unroll=8)

    def drain(a, c):
        pltpu.make_async_copy(_rows(src, 0, wr), _rows(out_ref, 0, wr), sem).wait()
        return c
    lax.fori_loop(0, n, drain, 0, unroll=8)


def _scatter_rows(words, dest3, fill_lo, fill_hi, n_rows, wr):
    steps, _, per = dest3.shape
    return pl.pallas_call(
        functools.partial(_scatter_kernel, wr=wr),
        out_shape=jax.ShapeDtypeStruct((n_rows * wr, LANES), jnp.uint32),
        grid_spec=pltpu.PrefetchScalarGridSpec(
            num_scalar_prefetch=2,
            grid=(steps,),
            in_specs=[pl.BlockSpec((1, 1, per), lambda i, lo, hi: (i, 0, 0), memory_space=pltpu.SMEM),
                      pl.BlockSpec((per // TOP_K * wr, LANES), lambda i, lo, hi: (i, 0))],
            out_specs=pl.BlockSpec(memory_space=pl.ANY),
            scratch_shapes=[pltpu.VMEM((wr, LANES), jnp.uint32),
                            pltpu.SemaphoreType.DMA(()), pltpu.SemaphoreType.DMA(())]),
        compiler_params=pltpu.CompilerParams(dimension_semantics=("arbitrary",), has_side_effects=True,
                                             disable_bounds_checks=True),
        name="moe_scatter_rows",
    )(fill_lo, fill_hi, dest3, words)


def _gather_rows(words, dest3, wr):
    steps, _, per = dest3.shape
    return pl.pallas_call(
        functools.partial(_gather_kernel, wr=wr),
        out_shape=jax.ShapeDtypeStruct((steps * per * wr, LANES), jnp.uint32),
        grid=(steps,),
        in_specs=[pl.BlockSpec((1, 1, per), lambda i: (i, 0, 0), memory_space=pltpu.SMEM),
                  pl.BlockSpec(memory_space=pl.ANY)],
        out_specs=pl.BlockSpec((per * wr, LANES), lambda i: (i, 0)),
        scratch_shapes=[pltpu.SemaphoreType.DMA(())],
        compiler_params=pltpu.CompilerParams(dimension_semantics=("arbitrary",),
                                             disable_bounds_checks=True),
        name="moe_gather_rows",
    )(dest3, words)


def _expert_gu_kernel(te_ref, first_ref, nreal_ref, rows_ref, wg_ref, wu_ref, bg_ref, bu_ref,
                      act_ref, wg_s, wu_s, *, wr):
    t = pl.program_id(1)
    tm = act_ref.shape[0]
    half = wr * LANES

    @pl.when(t < nreal_ref[0])
    def _():
        @pl.when(first_ref[t] == 1)
        def _():
            wg_s[...] = wg_ref[0].astype(BF16)
            wu_s[...] = wu_ref[0].astype(BF16)

        his, los = [], []
        for c in range(wr):
            hi, lo = _unpack_slab(rows_ref[pl.ds(c, tm, stride=wr), :])
            his.append(hi.astype(BF16))
            los.append(lo.astype(BF16))
        xa = jnp.concatenate(his, axis=1)
        xb = jnp.concatenate(los, axis=1)
        g = _dot(xa, wg_s[0:half]) + _dot(xb, wg_s[half:2 * half]) + bg_ref[0]
        u = _dot(xa, wu_s[0:half]) + _dot(xb, wu_s[half:2 * half]) + bu_ref[0]
        g = jnp.minimum(g, SWIGLU_LIMIT)
        u = jnp.clip(u, -SWIGLU_LIMIT, SWIGLU_LIMIT)
        act_ref[...] = (g * jax.nn.sigmoid(SWIGLU_ALPHA * g) * (u + 1.0)).astype(act_ref.dtype)

    @pl.when(t >= nreal_ref[0])
    def _():
        act_ref[...] = jnp.zeros_like(act_ref)


def _expert_gu(rows, w_gu, b_gu, tile_e, is_first, n_real, wr):
    n_exp, d, de2 = w_gu.shape
    de = de2 // 2
    tm = EXPERT_TILE
    n_tiles = rows.shape[0] // (tm * wr)
    tn = min(COL_TILE, de)
    nj = de // tn
    b3 = b_gu.reshape(n_exp, 1, de2)

    def tile(t, nr):
        return jnp.minimum(t, nr[0] - 1)

    return pl.pallas_call(
        functools.partial(_expert_gu_kernel, wr=wr),
        out_shape=jax.ShapeDtypeStruct((n_tiles * tm, de), BF16),
        grid_spec=pltpu.PrefetchScalarGridSpec(
            num_scalar_prefetch=3,
            grid=(nj, n_tiles),
            in_specs=[pl.BlockSpec((tm * wr, LANES), lambda j, t, te, fi, nr: (tile(t, nr), 0)),
                      pl.BlockSpec((1, d, tn), lambda j, t, te, fi, nr: (te[tile(t, nr)], 0, j)),
                      pl.BlockSpec((1, d, tn), lambda j, t, te, fi, nr: (te[tile(t, nr)], 0, nj + j)),
                      pl.BlockSpec((1, 1, tn), lambda j, t, te, fi, nr: (te[tile(t, nr)], 0, j)),
                      pl.BlockSpec((1, 1, tn), lambda j, t, te, fi, nr: (te[tile(t, nr)], 0, nj + j))],
            out_specs=pl.BlockSpec((tm, tn), lambda j, t, te, fi, nr: (t, j)),
            scratch_shapes=[pltpu.VMEM((d, tn), BF16), pltpu.VMEM((d, tn), BF16)]),
        compiler_params=_params("arbitrary", "arbitrary"),
        name="expert_gate_up",
    )(tile_e, is_first, n_real, rows, w_gu, w_gu, b3, b3)


def _expert_down_kernel(te_ref, first_ref, nreal_ref, act_ref, wd_ref, bd_ref, out_ref, wd_s):
    t = pl.program_id(0)
    tm = act_ref.shape[0]

    @pl.when(t < nreal_ref[0])
    def _():
        @pl.when(first_ref[t] == 1)
        def _():
            wd_s[...] = wd_ref[0].astype(BF16)

        _pack_rows(_dot(act_ref[...], wd_s[...]) + bd_ref[0], out_ref, tm)

    @pl.when(t >= nreal_ref[0])
    def _():
        out_ref[...] = jnp.zeros_like(out_ref)


def _expert_down(act, w_down, b_down, tile_e, is_first, n_real, wr):
    n_exp, de, d = w_down.shape
    tm = EXPERT_TILE
    n_tiles = act.shape[0] // tm

    def tile(t, nr):
        return jnp.minimum(t, nr[0] - 1)

    return pl.pallas_call(
        _expert_down_kernel,
        out_shape=jax.ShapeDtypeStruct((n_tiles * tm * wr, LANES), jnp.uint32),
        grid_spec=pltpu.PrefetchScalarGridSpec(
            num_scalar_prefetch=3,
            grid=(n_tiles,),
            in_specs=[pl.BlockSpec((tm, de), lambda t, te, fi, nr: (tile(t, nr), 0)),
                      pl.BlockSpec((1, de, d), lambda t, te, fi, nr: (te[tile(t, nr)], 0, 0)),
                      pl.BlockSpec((1, 1, d), lambda t, te, fi, nr: (te[tile(t, nr)], 0, 0))],
            out_specs=pl.BlockSpec((tm * wr, LANES), lambda t, te, fi, nr: (t, 0)),
            scratch_shapes=[pltpu.VMEM((de, d), BF16)]),
        compiler_params=_params("arbitrary"),
        name="expert_down",
    )(tile_e, is_first, n_real, act, w_down, b_down.reshape(n_exp, 1, d))


def _combine_kernel(*refs, wr, emit_h):
    y4_ref, gate_ref, x_ref, mod_ref = refs[:4]
    if emit_h:
        gain_ref, modn_ref, xo_ref, ho_ref = refs[4:]
    else:
        xo_ref, = refs[4:]
    tm = x_ref.shape[0]
    half = wr * LANES
    m = mod_ref[0]
    gate = gate_ref[...]
    gates = [gate[:, k:k + 1] for k in range(TOP_K)]
    for c in range(wr):
        acc_hi = jnp.zeros((tm, LANES), F32)
        acc_lo = jnp.zeros((tm, LANES), F32)
        for k in range(TOP_K):
            hi, lo = _unpack_slab(y4_ref[pl.ds(k * wr + c, tm, stride=TOP_K * wr), :])
            acc_hi = acc_hi + gates[k] * hi
            acc_lo = acc_lo + gates[k] * lo
        c0 = slice(c * LANES, (c + 1) * LANES)
        c1 = slice(half + c * LANES, half + (c + 1) * LANES)
        xo_ref[:, c0] = x_ref[:, c0] + m[5:6, c0] * acc_hi
        xo_ref[:, c1] = x_ref[:, c1] + m[5:6, c1] * acc_lo
    if emit_h:
        mn = modn_ref[0]
        ho_ref[...] = _rms_mod(xo_ref[...], gain_ref[...], mn[1:2], mn[0:1]).astype(ho_ref.dtype)


def _combine(y4, gate, x1, mod, seq, wr, next_gain=None, next_mod=None):
    n, d = x1.shape
    tm = min(ROW_TILE, seq)
    tpb = seq // tm
    emit_h = next_gain is not None
    row = lambda i: (i, 0)
    batch = lambda i: (i // tpb, 0, 0)
    args = [y4, gate, x1, mod]
    specs = [pl.BlockSpec((tm * TOP_K * wr, LANES), row), pl.BlockSpec((tm, LANES), row),
             pl.BlockSpec((tm, d), row), pl.BlockSpec((1, N_MOD, d), batch)]
    out_shape = [jax.ShapeDtypeStruct((n, d), F32)]
    out_specs = [pl.BlockSpec((tm, d), row)]
    if emit_h:
        args += [next_gain.reshape(1, d), next_mod]
        specs += [pl.BlockSpec((1, d), lambda i: (0, 0)), pl.BlockSpec((1, N_MOD, d), batch)]
        out_shape.append(jax.ShapeDtypeStruct((n, d), BF16))
        out_specs.append(pl.BlockSpec((tm, d), row))
    outs = pl.pallas_call(
        functools.partial(_combine_kernel, wr=wr, emit_h=emit_h),
        out_shape=tuple(out_shape),
        grid=(n // tm,),
        in_specs=specs,
        out_specs=tuple(out_specs),
        compiler_params=_params("arbitrary"),
        name="moe_combine",
    )(*args)
    return (outs[0], outs[1]) if emit_h else (outs[0], None)


def _routed_experts(x1, gain, mod, router_w, router_b, layer, w_gu, b_gu, w_down, b_down, seq,
                    next_gain, next_mod):
    n, d = x1.shape
    n_exp = router_w.shape[1]
    wr = d // (2 * LANES)
    words, idx, gate, rank, counts = _moe_pre(x1, gain, mod, router_w, router_b, seq)

    tm = EXPERT_TILE
    n_tiles = (n * TOP_K) // tm + n_exp
    cnt = counts[0, :n_exp].astype(jnp.int32)
    padded = (cnt + tm - 1) // tm * tm
    pad_end = jnp.cumsum(padded)
    pad_start = pad_end - padded
    onehot = idx[:, :TOP_K, None] == jnp.arange(n_exp, dtype=jnp.int32)
    dest = jnp.sum(jnp.where(onehot, pad_start, 0), axis=-1) + rank[:, :TOP_K]
    per = min(PERMUTE_TOKENS, n) * TOP_K
    dest3 = dest.reshape(n * TOP_K // per, 1, per)
    tile_start = jnp.arange(n_tiles, dtype=jnp.int32) * tm
    n_real = (pad_end[-1] // tm).astype(jnp.int32).reshape(1)
    tile_e = jnp.minimum(jnp.sum((tile_start[:, None] >= pad_end[None, :]).astype(jnp.int32), axis=1), n_exp - 1)
    is_first = (jnp.any(tile_start[:, None] == jnp.where(padded > 0, pad_start, -1)[None, :], axis=1)
                ).astype(jnp.int32)
    tile_e = tile_e + layer * n_exp

    fill_lo = jnp.concatenate([pad_start + cnt, pad_end[-1:]])
    fill_hi = jnp.concatenate([pad_end, jnp.full((1,), n_tiles * tm, jnp.int32)])
    rows = _scatter_rows(words, dest3, fill_lo, fill_hi, n_tiles * tm, wr)
    act = _expert_gu(rows, w_gu, b_gu, tile_e, is_first, n_real, wr)
    out_words = _expert_down(act, w_down, b_down, tile_e, is_first, n_real, wr)
    y4 = _gather_rows(out_words, dest3, wr)
    return _combine(y4, gate, x1, mod, seq, wr, next_gain, next_mod)


def kernel(x, c, w_ada, b_ada, ada_table, norm_mix, norm_ffn, sc_w_in, sc_w_conv, sc_w_out, pool_w, pool_scale, cf_w_pw1, cf_b_pw1, cf_w_dw, cf_b_dw, cf_ln_g, cf_ln_b, cf_w_pw2, cf_b_pw2, sb_w_qkv, sb_q_gain, sb_k_gain, sb_w_o, router_w, router_b, exp_w_gu, exp_b_gu, exp_w_down, exp_b_down):
    bsz, seq, d = x.shape
    depth = norm_mix.shape[0]
    n_mixers = len(POOL_WINDOWS)
    x2 = x.reshape(bsz * seq, d)
    mod_all = _ada(c, w_ada, b_ada, ada_table)
    n_exp = router_w.shape[-1]
    w_gu = exp_w_gu.reshape((depth * n_exp,) + exp_w_gu.shape[2:])
    b_gu = exp_b_gu.reshape((depth * n_exp,) + exp_b_gu.shape[2:])
    w_down = exp_w_down.reshape((depth * n_exp,) + exp_w_down.shape[2:])
    b_down = exp_b_down.reshape((depth * n_exp,) + exp_b_down.shape[2:])
    h = None
    for i in range(depth):
        kind, occ = i % n_mixers, i // n_mixers
        mod = mod_all[i]
        if kind != 1 and h is None:
            h = _norm_mod(x2, norm_mix[i], mod, seq)
        if kind == 0:
            g = _sc_in(h, sc_w_in[occ].astype(BF16), sc_w_conv[occ], seq)
            x1 = _proj_res(g, sc_w_out[occ].astype(BF16), x2, mod, seq)
        elif kind == 1:
            x1 = _pool_mixer(x2, norm_mix[i], mod, pool_w[occ].astype(BF16), pool_scale[occ], seq)
        elif kind == 2:
            u = _conf_in(h, cf_w_pw1[occ].astype(BF16), cf_b_pw1[occ], cf_w_dw[occ], cf_b_dw[occ], seq)
            x1 = _proj_res(u, cf_w_pw2[occ].astype(BF16), x2, mod, seq, bias=cf_b_pw2[occ],
                           ln=(cf_ln_g[occ], cf_ln_b[occ]))
        else:
            qkv = _qkv(h, sb_w_qkv[occ].astype(BF16), sb_q_gain[occ], sb_k_gain[occ], seq)
            o = _attention(qkv, bsz, seq)
            x1 = _proj_res(o, sb_w_o[occ].astype(BF16), x2, mod, seq)
        nxt = i + 1
        wants_h = nxt < depth and nxt % n_mixers != 1
        x2, h = _routed_experts(
            x1, norm_ffn[i], mod, router_w[i], router_b[i], i, w_gu, b_gu, w_down, b_down, seq,
            norm_mix[nxt] if wants_h else None, mod_all[nxt] if wants_h else None)
    return x2.reshape(bsz, seq, d)
```

```python
import functools

import jax
import jax.numpy as jnp
from jax import lax
from jax.experimental import pallas as pl
from jax.experimental.pallas import tpu as pltpu

EPS = 1e-6
LANES = 128
SUBLANES = 8
HEAD_DIM = 128
POOL_WINDOWS = (2, 4, 8, 16)
POOL_HALO = 16
TOP_K = 4
SWIGLU_LIMIT = 7.0
SWIGLU_ALPHA = 1.702
N_MOD = 6
VMEM_LIMIT_BYTES = 52 * 1024 * 1024
NEG_BIG = -1e30
CONV_ROWS = 32
ROW_TILE = 256
MM_TILE = 512
COL_TILE = 512
WEIGHT_CHUNK_ROWS = 512
EXPERT_TILE = 256
ATTN_TILE = 256
ATTN_KEYS = 256
ATTN_HEADS = 4
PERMUTE_TOKENS = 256

F32 = jnp.float32
BF16 = jnp.bfloat16


def _params(*sem):
    return pltpu.CompilerParams(dimension_semantics=sem, vmem_limit_bytes=VMEM_LIMIT_BYTES)


def _rms_mod(x, gain, scale, shift):
    ms = jnp.mean(x * x, axis=-1, keepdims=True)
    return x * lax.rsqrt(ms + EPS) * gain * (1.0 + scale) + shift


def _dot(a, b):
    return jnp.dot(a, b, preferred_element_type=F32)


def _pack_rows(val, ref, n_rows):
    half = val.shape[1] // 2
    wr = half // LANES
    bits = pltpu.bitcast(val.astype(BF16).astype(F32), jnp.uint32)
    word = (bits[:, :half] & jnp.uint32(0xFFFF0000)) | (bits[:, half:] >> 16)
    for c in range(wr):
        ref[pl.ds(c, n_rows, stride=wr), :] = word[:, c * LANES:(c + 1) * LANES]


def _unpack_slab(w):
    hi = pltpu.bitcast(w & jnp.uint32(0xFFFF0000), F32)
    lo = pltpu.bitcast(w << 16, F32)
    return hi, lo


def _ada_kernel(c_ref, w_ref, b_ref, t_ref, o_ref):
    c = c_ref[...]
    a = c * jax.nn.sigmoid(c)
    shared = jnp.dot(a, w_ref[...], preferred_element_type=F32,
                     precision=lax.Precision.HIGHEST) + b_ref[...]
    for i in range(o_ref.shape[0]):
        o_ref[i] = shared + t_ref[i:i + 1, :]


def _ada(c, w_ada, b_ada, ada_table):
    bsz, d = c.shape
    depth = ada_table.shape[0]
    n = w_ada.shape[1]
    c8 = jnp.zeros((SUBLANES, d), F32).at[:bsz].set(c)
    tn = min(COL_TILE, n)
    out = pl.pallas_call(
        _ada_kernel,
        out_shape=jax.ShapeDtypeStruct((depth, SUBLANES, n), F32),
        grid=(n // tn,),
        in_specs=[pl.BlockSpec((SUBLANES, d), lambda j: (0, 0)),
                  pl.BlockSpec((d, tn), lambda j: (0, j)),
                  pl.BlockSpec((1, tn), lambda j: (0, j)),
                  pl.BlockSpec((depth, tn), lambda j: (0, j))],
        out_specs=pl.BlockSpec((depth, SUBLANES, tn), lambda j: (0, 0, j)),
        compiler_params=_params("arbitrary"),
        name="ada_mod",
    )(c8, w_ada, b_ada.reshape(1, n), ada_table.reshape(depth, n))
    return out[:, :bsz].reshape(depth, bsz, N_MOD, d)


def _norm_mod_kernel(x_ref, g_ref, mod_ref, o_ref):
    m = mod_ref[0]
    o_ref[...] = _rms_mod(x_ref[...], g_ref[...], m[1:2], m[0:1]).astype(o_ref.dtype)


def _norm_mod(x2, gain, mod, seq):
    n, d = x2.shape
    tm = min(ROW_TILE, seq)
    tpb = seq // tm
    return pl.pallas_call(
        _norm_mod_kernel,
        out_shape=jax.ShapeDtypeStruct((n, d), BF16),
        grid=(n // tm,),
        in_specs=[pl.BlockSpec((tm, d), lambda i: (i, 0)),
                  pl.BlockSpec((1, d), lambda i: (0, 0)),
                  pl.BlockSpec((1, N_MOD, d), lambda i: (i // tpb, 0, 0))],
        out_specs=pl.BlockSpec((tm, d), lambda i: (i, 0)),
        compiler_params=_params("arbitrary"),
        name="norm_mod",
    )(x2, gain.reshape(1, d), mod)


def _causal_taps(val, buf, taps, halo, first):
    tm = val.shape[0]
    k_taps = taps.shape[0]

    @pl.when(first)
    def _():
        buf[0:halo] = jnp.zeros((halo, val.shape[1]), F32)

    @pl.when(jnp.logical_not(first))
    def _():
        buf[0:halo] = buf[tm:tm + halo]

    buf[halo:halo + tm] = val
    acc = taps[k_taps - 1:k_taps] * val
    for k in range(k_taps - 1):
        off = halo - (k_taps - 1) + k
        acc = acc + taps[k:k + 1] * buf[off:off + tm]
    return acc


def _causal_taps_wide(val, buf, shifted, taps_ref, bias, o_ref, halo, first):
    tm, tc = val.shape
    k_taps = taps_ref.shape[0]
    span = halo + tm - SUBLANES

    @pl.when(first)
    def _():
        buf[0:halo] = jnp.zeros((halo, tc), F32)

    @pl.when(jnp.logical_not(first))
    def _():
        buf[0:halo] = buf[tm:tm + halo]

    buf[halo:halo + tm] = val
    for r in range(1, SUBLANES):
        shifted[r - 1, 0:span] = buf[r:r + span]

    def chunk(ci, carry):
        r0 = pl.multiple_of(ci * CONV_ROWS, CONV_ROWS)
        acc = jnp.broadcast_to(bias, (CONV_ROWS, tc))
        for k in range(k_taps):
            off = halo - (k_taps - 1) + k
            r, base = off % SUBLANES, off - off % SUBLANES
            src = buf if r == 0 else shifted.at[r - 1]
            acc = acc + taps_ref[k:k + 1, :] * src[pl.ds(base + r0, CONV_ROWS), :]
        o_ref[pl.ds(r0, CONV_ROWS), :] = acc
        return carry
    lax.fori_loop(0, tm // CONV_ROWS, chunk, 0)


def _sc_in_kernel(h_ref, wb_ref, wc_ref, wu_ref, cw_ref, o_ref, buf, *, tpb, halo):
    i = pl.program_id(1)
    h = h_ref[...]
    b_gate = _dot(h, wb_ref[...])
    v = _dot(h, wc_ref[...]) * _dot(h, wu_ref[...])
    conv = _causal_taps(v, buf, cw_ref[...], halo, (i % tpb) == 0)
    o_ref[...] = (b_gate * conv).astype(o_ref.dtype)


def _sc_in(h, w_in, w_conv, seq):
    n, d = h.shape
    tm = min(MM_TILE, seq)
    tc = min(COL_TILE, d)
    nj = d // tc
    k_taps = w_conv.shape[0]
    halo = SUBLANES
    return pl.pallas_call(
        functools.partial(_sc_in_kernel, tpb=seq // tm, halo=halo),
        out_shape=jax.ShapeDtypeStruct((n, d), BF16),
        grid=(nj, n // tm),
        in_specs=[pl.BlockSpec((tm, d), lambda j, i: (i, 0)),
                  pl.BlockSpec((d, tc), lambda j, i: (0, j)),
                  pl.BlockSpec((d, tc), lambda j, i: (0, nj + j)),
                  pl.BlockSpec((d, tc), lambda j, i: (0, 2 * nj + j)),
                  pl.BlockSpec((k_taps, tc), lambda j, i: (0, j))],
        out_specs=pl.BlockSpec((tm, tc), lambda j, i: (i, j)),
        scratch_shapes=[pltpu.VMEM((halo + tm, tc), F32)],
        compiler_params=_params("arbitrary", "arbitrary"),
        name="short_conv_in",
    )(h, w_in, w_in, w_in, w_conv)


def _conf_in_kernel(h_ref, wa_ref, wg_ref, ba_ref, bg_ref, dw_ref, bdw_ref, o_ref, buf, shifted, *, tpb, halo):
    i = pl.program_id(1)
    h = h_ref[...]
    a = _dot(h, wa_ref[...]) + ba_ref[...]
    g = _dot(h, wg_ref[...]) + bg_ref[...]
    u = a * jax.nn.sigmoid(g)
    _causal_taps_wide(u, buf, shifted, dw_ref, bdw_ref[...], o_ref, halo, (i % tpb) == 0)


def _conf_in(h, w_pw1, b_pw1, w_dw, b_dw, seq):
    n, d = h.shape
    tm = min(MM_TILE, seq)
    tc = min(COL_TILE, d)
    nj = d // tc
    k_taps = w_dw.shape[0]
    halo = -(-(k_taps - 1) // SUBLANES) * SUBLANES
    b1 = b_pw1.reshape(1, 2 * d)
    return pl.pallas_call(
        functools.partial(_conf_in_kernel, tpb=seq // tm, halo=halo),
        out_shape=jax.ShapeDtypeStruct((n, d), F32),
        grid=(nj, n // tm),
        in_specs=[pl.BlockSpec((tm, d), lambda j, i: (i, 0)),
                  pl.BlockSpec((d, tc), lambda j, i: (0, j)),
                  pl.BlockSpec((d, tc), lambda j, i: (0, nj + j)),
                  pl.BlockSpec((1, tc), lambda j, i: (0, j)),
                  pl.BlockSpec((1, tc), lambda j, i: (0, nj + j)),
                  pl.BlockSpec((k_taps, tc), lambda j, i: (0, j)),
                  pl.BlockSpec((1, tc), lambda j, i: (0, j))],
        out_specs=pl.BlockSpec((tm, tc), lambda j, i: (i, j)),
        scratch_shapes=[pltpu.VMEM((halo + tm, tc), F32),
                        pltpu.VMEM((SUBLANES - 1, halo + tm - SUBLANES, tc), F32)],
        compiler_params=_params("arbitrary", "arbitrary"),
        name="conformer_in",
    )(h, w_pw1, w_pw1, b1, b1, w_dw, b_dw.reshape(1, d))


def _proj_res_kernel(*refs, has_bias, has_ln):
    a_ref, w_ref, x_ref, mod_ref = refs[:4]
    rest = list(refs[4:])
    b_ref = rest.pop(0) if has_bias else None
    lng_ref, lnb_ref = (rest.pop(0), rest.pop(0)) if has_ln else (None, None)
    o_ref = rest.pop(0)
    a = a_ref[...]
    if has_ln:
        mu = jnp.mean(a, axis=-1, keepdims=True)
        ac = a - mu
        var = jnp.mean(ac * ac, axis=-1, keepdims=True)
        a = ac * lax.rsqrt(var + EPS) * lng_ref[...] + lnb_ref[...]
        a = (a * jax.nn.sigmoid(a)).astype(BF16)
    y = _dot(a, w_ref[...])
    if has_bias:
        y = y + b_ref[...]
    o_ref[...] = x_ref[...] + mod_ref[0][2:3] * y


def _proj_res(a, w, x2, mod, seq, bias=None, ln=None):
    n, d = x2.shape
    k = a.shape[1]
    tm = min(ROW_TILE, seq)
    tpb = seq // tm
    row = lambda i: (i, 0)
    fixed = lambda i: (0, 0)
    args = [a, w, x2, mod]
    specs = [pl.BlockSpec((tm, k), row), pl.BlockSpec((k, d), fixed), pl.BlockSpec((tm, d), row),
             pl.BlockSpec((1, N_MOD, d), lambda i: (i // tpb, 0, 0))]
    if bias is not None:
        args.append(bias.reshape(1, d))
        specs.append(pl.BlockSpec((1, d), fixed))
    if ln is not None:
        args += [ln[0].reshape(1, k), ln[1].reshape(1, k)]
        specs += [pl.BlockSpec((1, k), fixed), pl.BlockSpec((1, k), fixed)]
    return pl.pallas_call(
        functools.partial(_proj_res_kernel, has_bias=bias is not None, has_ln=ln is not None),
        out_shape=jax.ShapeDtypeStruct((n, d), F32),
        grid=(n // tm,),
        in_specs=specs,
        out_specs=pl.BlockSpec((tm, d), row),
        compiler_params=_params("arbitrary"),
        name="proj_residual",
    )(*args)


def _pool_kernel(x_ref, halo_ref, gain_ref, mod_ref, w_ref, ps_ref, o_ref, buf, *, tpb):
    i = pl.program_id(0)
    tm = x_ref.shape[0]
    grp = w_ref.shape[1]
    m = mod_ref[0]
    gain = gain_ref[...]
    x = x_ref[...]
    h = _rms_mod(x, gain, m[1:2], m[0:1])
    hh = _rms_mod(halo_ref[...], gain, m[1:2], m[0:1])
    tile_in_seq = i % tpb
    buf[0:POOL_HALO] = jnp.where(tile_in_seq == 0, 0.0, hh)
    buf[POOL_HALO:POOL_HALO + tm] = h
    pos = tile_in_seq * tm + lax.broadcasted_iota(jnp.int32, (tm, 1), 0)
    for g, win in enumerate(POOL_WINDOWS):
        cs = slice(g * grp, (g + 1) * grp)
        hg = h[:, cs]
        acc = hg
        for j in range(1, win):
            acc = acc + buf[POOL_HALO - j:POOL_HALO - j + tm, cs]
        cnt = jnp.minimum(pos + 1, win).astype(F32)
        pooled = (acc / cnt - hg).astype(BF16)
        y = _dot(pooled, w_ref[g]) * ps_ref[:, cs]
        o_ref[:, cs] = x[:, cs] + m[2:3, cs] * y


def _pool_mixer(x2, gain, mod, pool_w, pool_scale, seq):
    n, d = x2.shape
    tm = min(ROW_TILE, seq)
    tpb = seq // tm
    hb = tm // POOL_HALO
    ng, grp = pool_w.shape[0], pool_w.shape[1]
    return pl.pallas_call(
        functools.partial(_pool_kernel, tpb=tpb),
        out_shape=jax.ShapeDtypeStruct((n, d), F32),
        grid=(n // tm,),
        in_specs=[pl.BlockSpec((tm, d), lambda i: (i, 0)),
                  pl.BlockSpec((POOL_HALO, d), lambda i: (jnp.maximum(i * hb - 1, 0), 0)),
                  pl.BlockSpec((1, d), lambda i: (0, 0)),
                  pl.BlockSpec((1, N_MOD, d), lambda i: (i // tpb, 0, 0)),
                  pl.BlockSpec((ng, grp, grp), lambda i: (0, 0, 0)),
                  pl.BlockSpec((1, d), lambda i: (0, 0))],
        out_specs=pl.BlockSpec((tm, d), lambda i: (i, 0)),
        scratch_shapes=[pltpu.VMEM((POOL_HALO + tm, d), F32)],
        compiler_params=_params("arbitrary"),
        name="pool_mixer",
    )(x2, x2, gain.reshape(1, d), mod, pool_w, pool_scale.reshape(1, d))


def _qkv_kernel(h_ref, w_ref, gains_ref, o_ref, *, nper):
    j = pl.program_id(1)
    y = _dot(h_ref[...], w_ref[...])
    part = j // nper
    heads = y.shape[1] // HEAD_DIM

    def normed(gain, extra):
        outs = []
        for hh in range(heads):
            yh = y[:, hh * HEAD_DIM:(hh + 1) * HEAD_DIM]
            ms = jnp.mean(yh * yh, axis=-1, keepdims=True)
            outs.append(yh * lax.rsqrt(ms + EPS) * gain * extra)
        return jnp.concatenate(outs, axis=1).astype(o_ref.dtype)

    @pl.when(part == 0)
    def _():
        o_ref[...] = normed(gains_ref[0:1, :], HEAD_DIM ** -0.5)

    @pl.when(part == 1)
    def _():
        o_ref[...] = normed(gains_ref[1:2, :], 1.0)

    @pl.when(part == 2)
    def _():
        o_ref[...] = y.astype(o_ref.dtype)


def _qkv(h, w_qkv, q_gain, k_gain, seq):
    n, d = h.shape
    tm = min(MM_TILE, seq)
    tn = min(COL_TILE, d)
    gains = jnp.zeros((SUBLANES, HEAD_DIM), F32).at[0].set(q_gain).at[1].set(k_gain)
    return pl.pallas_call(
        functools.partial(_qkv_kernel, nper=d // tn),
        out_shape=jax.ShapeDtypeStruct((n, 3 * d), BF16),
        grid=(n // tm, 3 * d // tn),
        in_specs=[pl.BlockSpec((tm, d), lambda i, j: (i, 0)),
                  pl.BlockSpec((d, tn), lambda i, j: (0, j)),
                  pl.BlockSpec((SUBLANES, HEAD_DIM), lambda i, j: (0, 0))],
        out_specs=pl.BlockSpec((tm, tn), lambda i, j: (i, j)),
        compiler_params=_params("arbitrary", "arbitrary"),
        name="qkv_proj",
    )(h, w_qkv, gains)


def _attn_kernel(q_ref, k_ref, v_ref, o_ref):
    qi = pl.program_id(2)
    tq = q_ref.shape[0]
    tk = min(ATTN_KEYS, k_ref.shape[0])
    heads = q_ref.shape[1] // HEAD_DIM
    diag = (qi * tq) // tk
    suffix = (lax.broadcasted_iota(jnp.int32, (tk, tk), 0)
              >= lax.broadcasted_iota(jnp.int32, (tk, tk), 1)).astype(BF16)
    valid = (diag * tk + lax.broadcasted_iota(jnp.int32, (tq, tk), 1)
             < qi * tq + lax.broadcasted_iota(jnp.int32, (tq, tk), 0))
    qs = [q_ref[:, hh * HEAD_DIM:(hh + 1) * HEAD_DIM] for hh in range(heads)]

    def blocks(kb, laters, masked):
        start = pl.multiple_of(kb * tk, tk)
        hs = range(heads)
        cols = [slice(hh * HEAD_DIM, (hh + 1) * HEAD_DIM) for hh in hs]
        zs = [lax.dot_general(qs[hh], k_ref[pl.ds(start, tk), cols[hh]], (((1,), (1,)), ((), ())),
                              preferred_element_type=F32) for hh in hs]
        csums = []
        for z in zs:
            neg_abs = pltpu.bitcast(pltpu.bitcast(z, jnp.uint32) | jnp.uint32(0x80000000), F32)
            sp = jnp.maximum(z, 0.0) + jnp.log(1.0 + jnp.exp(neg_abs))
            if masked:
                sp = jnp.where(valid, sp, 0.0)
            csums.append(_dot(sp.astype(BF16), suffix))
        outs = []
        for hh in hs:
            a = jnp.exp(zs[hh] - (csums[hh] + laters[hh]))
            if masked:
                a = jnp.where(valid, a, 0.0)
            outs.append(_dot(a.astype(BF16), v_ref[pl.ds(start, tk), cols[hh]]))
        return outs, [laters[hh] + csums[hh][:, 0:1] for hh in hs]

    zero = jnp.zeros((tq, 1), F32)
    accs, laters = blocks(diag, [zero] * heads, True)

    def body(it, carry):
        accs, laters = carry
        outs, laters = blocks(diag - 1 - it, laters, False)
        return [acc + out for acc, out in zip(accs, outs)], laters

    accs, laters = lax.fori_loop(0, diag, body, (accs, laters))
    o_ref[...] = jnp.concatenate(accs, axis=1).astype(o_ref.dtype)


def _attention(qkv, bsz, seq):
    n, d3 = qkv.shape
    d = d3 // 3
    hw = ATTN_HEADS * HEAD_DIM
    ng = d // hw
    tq = min(ATTN_TILE, seq)
    nq = seq // tq
    return pl.pallas_call(
        _attn_kernel,
        out_shape=jax.ShapeDtypeStruct((n, d), BF16),
        grid=(bsz, ng, nq),
        in_specs=[pl.BlockSpec((tq, hw), lambda b, h, i: (b * nq + i, h)),
                  pl.BlockSpec((seq, hw), lambda b, h, i: (b, ng + h)),
                  pl.BlockSpec((seq, hw), lambda b, h, i: (b, 2 * ng + h))],
        out_specs=pl.BlockSpec((tq, hw), lambda b, h, i: (b * nq + i, h)),
        compiler_params=_params("arbitrary", "arbitrary", "arbitrary"),
        name="stick_breaking_attn",
    )(qkv, qkv, qkv)


def _moe_pre_kernel(x_ref, gain_ref, mod_ref, rw_ref, rb_ref,
                    words_ref, idx_ref, gate_ref, rank_ref, cnt_ref, run):
    i = pl.program_id(0)
    tm = x_ref.shape[0]

    @pl.when(i == 0)
    def _():
        run[...] = jnp.zeros_like(run)

    m = mod_ref[0]
    h = _rms_mod(x_ref[...], gain_ref[...], m[4:5], m[3:4])
    _pack_rows(h, words_ref, tm)

    logits = jnp.dot(h, rw_ref[...], preferred_element_type=F32,
                     precision=lax.Precision.HIGHEST) + rb_ref[...]
    lane = lax.broadcasted_iota(jnp.int32, (tm, LANES), 1)
    onehots, vals, idxs = [], [], []
    cur = logits
    for _ in range(TOP_K):
        mx = jnp.max(cur, axis=-1, keepdims=True)
        ix = jnp.min(jnp.where(cur == mx, lane, LANES), axis=-1, keepdims=True)
        oh = lane == ix
        onehots.append(oh)
        vals.append(mx)
        idxs.append(ix)
        cur = jnp.where(oh, -jnp.inf, cur)
    exps = [jnp.exp(v - vals[0]) for v in vals]
    den = exps[0]
    for e in exps[1:]:
        den = den + e

    ohs = jnp.concatenate([oh.astype(BF16) for oh in onehots], axis=1)
    r_i = lax.broadcasted_iota(jnp.int32, (tm, tm), 0)
    c_i = lax.broadcasted_iota(jnp.int32, (tm, tm), 1)
    earlier = _dot((c_i < r_i).astype(BF16), ohs)
    base = run[...]
    idx_out = jnp.zeros((tm, LANES), jnp.int32)
    rank_out = jnp.zeros((tm, LANES), jnp.int32)
    gate_out = jnp.zeros((tm, LANES), F32)
    for k in range(TOP_K):
        ohf = onehots[k].astype(F32)
        rk = jnp.sum(ohf * (earlier[:, k * LANES:(k + 1) * LANES] + base), axis=-1, keepdims=True)
        base = base + jnp.sum(ohf, axis=0, keepdims=True)
        idx_out = jnp.where(lane == k, idxs[k], idx_out)
        rank_out = jnp.where(lane == k, rk.astype(jnp.int32), rank_out)
        gate_out = jnp.where(lane == k, exps[k] / den, gate_out)
    run[...] = base
    cnt_ref[...] = base
    idx_ref[...] = idx_out
    rank_ref[...] = rank_out
    gate_ref[...] = gate_out


def _moe_pre(x1, gain, mod, router_w, router_b, seq):
    n, d = x1.shape
    n_exp = router_w.shape[1]
    tm = min(ROW_TILE, seq)
    tpb = seq // tm
    wr = d // (2 * LANES)
    rw = jnp.zeros((d, LANES), F32).at[:, :n_exp].set(router_w)
    rb = jnp.full((1, LANES), NEG_BIG, F32).at[0, :n_exp].set(router_b)
    row = lambda i: (i, 0)
    fixed = lambda i: (0, 0)
    return pl.pallas_call(
        _moe_pre_kernel,
        out_shape=(jax.ShapeDtypeStruct((n * wr, LANES), jnp.uint32),
                   jax.ShapeDtypeStruct((n, LANES), jnp.int32),
                   jax.ShapeDtypeStruct((n, LANES), F32),
                   jax.ShapeDtypeStruct((n, LANES), jnp.int32),
                   jax.ShapeDtypeStruct((1, LANES), F32)),
        grid=(n // tm,),
        in_specs=[pl.BlockSpec((tm, d), row),
                  pl.BlockSpec((1, d), fixed),
                  pl.BlockSpec((1, N_MOD, d), lambda i: (i // tpb, 0, 0)),
                  pl.BlockSpec((d, LANES), fixed),
                  pl.BlockSpec((1, LANES), fixed)],
        out_specs=(pl.BlockSpec((tm * wr, LANES), row),
                   pl.BlockSpec((tm, LANES), row),
                   pl.BlockSpec((tm, LANES), row),
                   pl.BlockSpec((tm, LANES), row),
                   pl.BlockSpec((1, LANES), fixed)),
        scratch_shapes=[pltpu.VMEM((1, LANES), F32)],
        compiler_params=_params("arbitrary"),
        name="moe_route",
    )(x1, gain.reshape(1, d), mod, rw, rb)


def _rows(ref, row, wr):
    return ref.at[pl.ds(pl.multiple_of(row * wr, wr), wr), :]


def _scatter_kernel(fill_lo_ref, fill_hi_ref, dest_ref, src_ref, dst, zbuf, sem, zsem, *, wr):
    i = pl.program_id(0)
    n = dest_ref.shape[-1]

    @pl.when(i == 0)
    def _():
        zbuf[...] = jnp.zeros_like(zbuf)

        def per_range(e, count):
            def per_row(r, c):
                pltpu.make_async_copy(zbuf, _rows(dst, r, wr), zsem).start()
                return c + 1
            return lax.fori_loop(fill_lo_ref[e], fill_hi_ref[e], per_row, count)

        total = lax.fori_loop(0, fill_lo_ref.shape[0], per_range, 0)

        def drain(r, c):
            pltpu.make_async_copy(zbuf, _rows(dst, 0, wr), zsem).wait()
            return c
        lax.fori_loop(0, total, drain, 0)

    def issue(a, c):
        token = lax.shift_right_logical(a, TOP_K.bit_length() - 1)
        pltpu.make_async_copy(_rows(src_ref, token, wr), _rows(dst, dest_ref[0, 0, a], wr), sem).start()
        return c
    lax.fori_loop(0, n, issue, 0, unroll=8)

    def drain(a, c):
        pltpu.make_async_copy(_rows(src_ref, 0, wr), _rows(dst, 0, wr), sem).wait()
        return c
    lax.fori_loop(0, n, drain, 0, unroll=8)


def _gather_kernel(dest_ref, src, out_ref, sem, *, wr):
    n = dest_ref.shape[-1]

    def issue(a, c):
        pltpu.make_async_copy(_rows(src, dest_ref[0, 0, a], wr), _rows(out_ref, a, wr), sem).start()
        return c
    lax.fori_loop(0, n, issue, 0, unroll=8)

    def drain(a, c):
        pltpu.make_async_copy(_rows(src, 0, wr), _rows(out_ref, 0, wr), sem).wait()
        return c
    lax.fori_loop(0, n, drain, 0, unroll=8)


def _scatter_rows(words, dest3, fill_lo, fill_hi, n_rows, wr):
    steps, _, per = dest3.shape
    return pl.pallas_call(
        functools.partial(_scatter_kernel, wr=wr),
        out_shape=jax.ShapeDtypeStruct((n_rows * wr, LANES), jnp.uint32),
        grid_spec=pltpu.PrefetchScalarGridSpec(
            num_scalar_prefetch=2,
            grid=(steps,),
            in_specs=[pl.BlockSpec((1, 1, per), lambda i, lo, hi: (i, 0, 0), memory_space=pltpu.SMEM),
                      pl.BlockSpec((per // TOP_K * wr, LANES), lambda i, lo, hi: (i, 0))],
            out_specs=pl.BlockSpec(memory_space=pl.ANY),
            scratch_shapes=[pltpu.VMEM((wr, LANES), jnp.uint32),
                            pltpu.SemaphoreType.DMA(()), pltpu.SemaphoreType.DMA(())]),
        compiler_params=pltpu.CompilerParams(dimension_semantics=("arbitrary",), has_side_effects=True,
                                             disable_bounds_checks=True),
        name="moe_scatter_rows",
    )(fill_lo, fill_hi, dest3, words)


def _gather_rows(words, dest3, wr):
    steps, _, per = dest3.shape
    return pl.pallas_call(
        functools.partial(_gather_kernel, wr=wr),
        out_shape=jax.ShapeDtypeStruct((steps * per * wr, LANES), jnp.uint32),
        grid=(steps,),
        in_specs=[pl.BlockSpec((1, 1, per), lambda i: (i, 0, 0), memory_space=pltpu.SMEM),
                  pl.BlockSpec(memory_space=pl.ANY)],
        out_specs=pl.BlockSpec((per * wr, LANES), lambda i: (i, 0)),
        scratch_shapes=[pltpu.SemaphoreType.DMA(())],
        compiler_params=pltpu.CompilerParams(dimension_semantics=("arbitrary",),
                                             disable_bounds_checks=True),
        name="moe_gather_rows",
    )(dest3, words)


def _expert_ffn_kernel(te_ref, first_ref, slot_ref, next_ref, nreal_ref,
                       rows_ref, bgu_ref, bd_ref, wgu_hbm, wd_hbm, out_ref,
                       wbuf, stage, sems, done, *, wr, n_gu, n_down):
    t = pl.program_id(0)
    tm = rows_ref.shape[0] // wr
    half = wr * LANES
    n_chunks = n_gu + n_down
    ch = stage.shape[1]
    rows_gu = n_gu * ch
    d_exp = wd_hbm.shape[1] * n_down

    def start(expert, c):
        @pl.when(c < n_gu)
        def _():
            pltpu.make_async_copy(wgu_hbm.at[expert * n_gu + c], stage.at[c % 2], sems.at[c % 2]).start()

        @pl.when(c >= n_gu)
        def _():
            pltpu.make_async_copy(wd_hbm.at[expert * n_down + (c - n_gu)], stage.at[c % 2], sems.at[c % 2]).start()

    def advance(expert, slot, upto):
        def body(c, carry):
            pltpu.make_async_copy(wgu_hbm.at[0], stage.at[c % 2], sems.at[c % 2]).wait()
            wbuf[slot, pl.ds(pl.multiple_of(c * ch, ch), ch), :] = stage[c % 2].astype(BF16)

            @pl.when(c + 2 < n_chunks)
            def _():
                start(expert, c + 2)
            return carry
        lax.fori_loop(done[0], upto, body, 0)
        done[0] = jnp.maximum(done[0], upto)

    @pl.when(t < nreal_ref[0])
    def _():
        expert, slot, nxt = te_ref[t], slot_ref[t], next_ref[t]

        @pl.when(first_ref[t] == 1)
        def _():
            @pl.when(t == 0)
            def _():
                done[0] = 0
                start(expert, 0)
                start(expert, 1)

            advance(expert, slot, n_chunks)
            done[0] = 0

            @pl.when(nxt >= 0)
            def _():
                start(nxt, 0)
                start(nxt, 1)

        his, los = [], []
        for c in range(wr):
            hi, lo = _unpack_slab(rows_ref[pl.ds(c, tm, stride=wr), :])
            his.append(hi.astype(BF16))
            los.append(lo.astype(BF16))
        xa = jnp.concatenate(his, axis=1)
        xb = jnp.concatenate(los, axis=1)
        gu = _dot(xa, wbuf[slot, 0:half, :]) + _dot(xb, wbuf[slot, half:2 * half, :]) + bgu_ref[0]
        g = jnp.minimum(gu[:, :d_exp], SWIGLU_LIMIT)
        u = jnp.clip(gu[:, d_exp:], -SWIGLU_LIMIT, SWIGLU_LIMIT)
        act = (g * jax.nn.sigmoid(SWIGLU_ALPHA * g) * (u + 1.0)).astype(BF16)
        _pack_rows(_dot(act, wbuf[slot, rows_gu:rows_gu + d_exp, :]) + bd_ref[0], out_ref, tm)

        @pl.when(nxt >= 0)
        def _():
            advance(nxt, 1 - slot, jnp.minimum(done[0] + 2, n_chunks))

    @pl.when(t >= nreal_ref[0])
    def _():
        out_ref[...] = jnp.zeros_like(out_ref)


def _expert_ffn(rows, w_gu, b_gu, w_down, b_down, tile_e, is_first, tile_slot, tile_next, n_real, wr):
    n_exp, d, de2 = w_gu.shape
    de = w_down.shape[1]
    assert de2 == 2 * de and w_down.shape[2] == d and de2 == d, "gate/up and down chunks share one staging shape"
    ch = min(WEIGHT_CHUNK_ROWS, de)
    n_gu, n_down = d // ch, de // ch
    tm = EXPERT_TILE
    n_tiles = rows.shape[0] // (tm * wr)

    def tile(t, nr):
        return jnp.minimum(t, nr[0] - 1)

    return pl.pallas_call(
        functools.partial(_expert_ffn_kernel, wr=wr, n_gu=n_gu, n_down=n_down),
        out_shape=jax.ShapeDtypeStruct((n_tiles * tm * wr, LANES), jnp.uint32),
        grid_spec=pltpu.PrefetchScalarGridSpec(
            num_scalar_prefetch=5,
            grid=(n_tiles,),
            in_specs=[pl.BlockSpec((tm * wr, LANES), lambda t, te, fi, sl, nx, nr: (tile(t, nr), 0)),
                      pl.BlockSpec((1, 1, de2), lambda t, te, fi, sl, nx, nr: (te[tile(t, nr)], 0, 0)),
                      pl.BlockSpec((1, 1, d), lambda t, te, fi, sl, nx, nr: (te[tile(t, nr)], 0, 0)),
                      pl.BlockSpec(memory_space=pl.ANY),
                      pl.BlockSpec(memory_space=pl.ANY)],
            out_specs=pl.BlockSpec((tm * wr, LANES), lambda t, te, fi, sl, nx, nr: (t, 0)),
            scratch_shapes=[pltpu.VMEM((2, d + de, d), BF16),
                            pltpu.VMEM((2, ch, d), F32),
                            pltpu.SemaphoreType.DMA((2,)),
                            pltpu.SMEM((1,), jnp.int32)]),
        compiler_params=_params("arbitrary"),
        name="expert_ffn",
    )(tile_e, is_first, tile_slot, tile_next, n_real, rows,
      b_gu.reshape(n_exp, 1, de2), b_down.reshape(n_exp, 1, d),
      w_gu.reshape(n_exp * n_gu, ch, de2), w_down.reshape(n_exp * n_down, ch, d))


def _combine_kernel(*refs, wr, emit_h):
    y4_ref, gate_ref, x_ref, mod_ref = refs[:4]
    if emit_h:
        gain_ref, modn_ref, xo_ref, ho_ref = refs[4:]
    else:
        xo_ref, = refs[4:]
    tm = x_ref.shape[0]
    half = wr * LANES
    m = mod_ref[0]
    gate = gate_ref[...]
    gates = [gate[:, k:k + 1] for k in range(TOP_K)]
    for c in range(wr):
        acc_hi = jnp.zeros((tm, LANES), F32)
        acc_lo = jnp.zeros((tm, LANES), F32)
        for k in range(TOP_K):
            hi, lo = _unpack_slab(y4_ref[pl.ds(k * wr + c, tm, stride=TOP_K * wr), :])
            acc_hi = acc_hi + gates[k] * hi
            acc_lo = acc_lo + gates[k] * lo
        c0 = slice(c * LANES, (c + 1) * LANES)
        c1 = slice(half + c * LANES, half + (c + 1) * LANES)
        xo_ref[:, c0] = x_ref[:, c0] + m[5:6, c0] * acc_hi
        xo_ref[:, c1] = x_ref[:, c1] + m[5:6, c1] * acc_lo
    if emit_h:
        mn = modn_ref[0]
        ho_ref[...] = _rms_mod(xo_ref[...], gain_ref[...], mn[1:2], mn[0:1]).astype(ho_ref.dtype)


def _combine(y4, gate, x1, mod, seq, wr, next_gain=None, next_mod=None):
    n, d = x1.shape
    tm = min(ROW_TILE, seq)
    tpb = seq // tm
    emit_h = next_gain is not None
    row = lambda i: (i, 0)
    batch = lambda i: (i // tpb, 0, 0)
    args = [y4, gate, x1, mod]
    specs = [pl.BlockSpec((tm * TOP_K * wr, LANES), row), pl.BlockSpec((tm, LANES), row),
             pl.BlockSpec((tm, d), row), pl.BlockSpec((1, N_MOD, d), batch)]
    out_shape = [jax.ShapeDtypeStruct((n, d), F32)]
    out_specs = [pl.BlockSpec((tm, d), row)]
    if emit_h:
        args += [next_gain.reshape(1, d), next_mod]
        specs += [pl.BlockSpec((1, d), lambda i: (0, 0)), pl.BlockSpec((1, N_MOD, d), batch)]
        out_shape.append(jax.ShapeDtypeStruct((n, d), BF16))
        out_specs.append(pl.BlockSpec((tm, d), row))
    outs = pl.pallas_call(
        functools.partial(_combine_kernel, wr=wr, emit_h=emit_h),
        out_shape=tuple(out_shape),
        grid=(n // tm,),
        in_specs=specs,
        out_specs=tuple(out_specs),
        compiler_params=_params("arbitrary"),
        name="moe_combine",
    )(*args)
    return (outs[0], outs[1]) if emit_h else (outs[0], None)


def _routed_experts(x1, gain, mod, router_w, router_b, layer, w_gu, b_gu, w_down, b_down, seq,
                    next_gain, next_mod):
    n, d = x1.shape
    n_exp = router_w.shape[1]
    wr = d // (2 * LANES)
    words, idx, gate, rank, counts = _moe_pre(x1, gain, mod, router_w, router_b, seq)

    tm = EXPERT_TILE
    n_tiles = (n * TOP_K) // tm + n_exp
    cnt = counts[0, :n_exp].astype(jnp.int32)
    padded = (cnt + tm - 1) // tm * tm
    pad_end = jnp.cumsum(padded)
    pad_start = pad_end - padded
    onehot = idx[:, :TOP_K, None] == jnp.arange(n_exp, dtype=jnp.int32)
    dest = jnp.sum(jnp.where(onehot, pad_start, 0), axis=-1) + rank[:, :TOP_K]
    per = min(PERMUTE_TOKENS, n) * TOP_K
    dest3 = dest.reshape(n * TOP_K // per, 1, per)
    tile_start = jnp.arange(n_tiles, dtype=jnp.int32) * tm
    n_real = (pad_end[-1] // tm).astype(jnp.int32).reshape(1)
    tile_e = jnp.minimum(jnp.sum((tile_start[:, None] >= pad_end[None, :]).astype(jnp.int32), axis=1), n_exp - 1)
    is_first = (jnp.any(tile_start[:, None] == jnp.where(padded > 0, pad_start, -1)[None, :], axis=1)
                ).astype(jnp.int32)
    ids = jnp.arange(n_exp, dtype=jnp.int32)
    used = padded > 0
    group_slot = (jnp.cumsum(used.astype(jnp.int32)) - 1) % 2
    later_used = jnp.where(used[None, :] & (ids[None, :] > ids[:, None]), ids[None, :], n_exp)
    next_used = jnp.min(later_used, axis=1)
    group_next = jnp.where(next_used < n_exp, next_used + layer * n_exp, -1)
    tile_slot = group_slot[tile_e].astype(jnp.int32)
    tile_next = group_next[tile_e].astype(jnp.int32)
    tile_e = tile_e + layer * n_exp

    fill_lo = jnp.concatenate([pad_start + cnt, pad_end[-1:]])
    fill_hi = jnp.concatenate([pad_end, jnp.full((1,), n_tiles * tm, jnp.int32)])
    rows = _scatter_rows(words, dest3, fill_lo, fill_hi, n_tiles * tm, wr)
    out_words = _expert_ffn(rows, w_gu, b_gu, w_down, b_down, tile_e, is_first, tile_slot, tile_next, n_real, wr)
    y4 = _gather_rows(out_words, dest3, wr)
    return _combine(y4, gate, x1, mod, seq, wr, next_gain, next_mod)


def kernel(x, c, w_ada, b_ada, ada_table, norm_mix, norm_ffn, sc_w_in, sc_w_conv, sc_w_out, pool_w, pool_scale, cf_w_pw1, cf_b_pw1, cf_w_dw, cf_b_dw, cf_ln_g, cf_ln_b, cf_w_pw2, cf_b_pw2, sb_w_qkv, sb_q_gain, sb_k_gain, sb_w_o, router_w, router_b, exp_w_gu, exp_b_gu, exp_w_down, exp_b_down):
    bsz, seq, d = x.shape
    depth = norm_mix.shape[0]
    n_mixers = len(POOL_WINDOWS)
    x2 = x.reshape(bsz * seq, d)
    mod_all = _ada(c, w_ada, b_ada, ada_table)
    n_exp = router_w.shape[-1]
    w_gu = exp_w_gu.reshape((depth * n_exp,) + exp_w_gu.shape[2:])
    b_gu = exp_b_gu.reshape((depth * n_exp,) + exp_b_gu.shape[2:])
    w_down = exp_w_down.reshape((depth * n_exp,) + exp_w_down.shape[2:])
    b_down = exp_b_down.reshape((depth * n_exp,) + exp_b_down.shape[2:])
    h = None
    for i in range(depth):
        kind, occ = i % n_mixers, i // n_mixers
        mod = mod_all[i]
        if kind != 1 and h is None:
            h = _norm_mod(x2, norm_mix[i], mod, seq)
        if kind == 0:
            g = _sc_in(h, sc_w_in[occ].astype(BF16), sc_w_conv[occ], seq)
            x1 = _proj_res(g, sc_w_out[occ].astype(BF16), x2, mod, seq)
        elif kind == 1:
            x1 = _pool_mixer(x2, norm_mix[i], mod, pool_w[occ].astype(BF16), pool_scale[occ], seq)
        elif kind == 2:
            u = _conf_in(h, cf_w_pw1[occ].astype(BF16), cf_b_pw1[occ], cf_w_dw[occ], cf_b_dw[occ], seq)
            x1 = _proj_res(u, cf_w_pw2[occ].astype(BF16), x2, mod, seq, bias=cf_b_pw2[occ],
                           ln=(cf_ln_g[occ], cf_ln_b[occ]))
        else:
            qkv = _qkv(h, sb_w_qkv[occ].astype(BF16), sb_q_gain[occ], sb_k_gain[occ], seq)
            o = _attention(qkv, bsz, seq)
            x1 = _proj_res(o, sb_w_o[occ].astype(BF16), x2, mod, seq)
        nxt = i + 1
        wants_h = nxt < depth and nxt % n_mixers != 1
        x2, h = _routed_experts(
            x1, norm_ffn[i], mod, router_w[i], router_b[i], i, w_gu, b_gu, w_down, b_down, seq,
            norm_mix[nxt] if wants_h else None, mod_all[nxt] if wants_h else None)
    return x2.reshape(bsz, seq, d)
```

```python
import functools

import jax
import jax.numpy as jnp
from jax import lax
from jax.experimental import pallas as pl
from jax.experimental.pallas import tpu as pltpu

EPS = 1e-6
LANES = 128
SUBLANES = 8
HEAD_DIM = 128
POOL_WINDOWS = (2, 4, 8, 16)
POOL_HALO = 16
TOP_K = 4
SWIGLU_LIMIT = 7.0
SWIGLU_ALPHA = 1.702
N_MOD = 6
VMEM_LIMIT_BYTES = 52 * 1024 * 1024
NEG_BIG = -1e30
CONV_ROWS = 32
ROW_TILE = 256
MM_TILE = 512
COL_TILE = 512
WEIGHT_CHUNK_ROWS = 512
EXPERT_TILE = 256
ATTN_TILE = 256
ATTN_KEYS = 256
ATTN_HEADS = 4
ATTN_UNROLL = 4

F32 = jnp.float32
BF16 = jnp.bfloat16


def _params(*sem):
    return pltpu.CompilerParams(dimension_semantics=sem, vmem_limit_bytes=VMEM_LIMIT_BYTES)


def _rms_mod(x, gain, scale, shift):
    ms = jnp.mean(x * x, axis=-1, keepdims=True)
    return x * lax.rsqrt(ms + EPS) * gain * (1.0 + scale) + shift


def _dot(a, b):
    return jnp.dot(a, b, preferred_element_type=F32)


def _pack_words(val):
    half = val.shape[1] // 2
    bits = pltpu.bitcast(val.astype(BF16).astype(F32), jnp.uint32)
    return (bits[:, :half] & jnp.uint32(0xFFFF0000)) | (bits[:, half:] >> 16)


def _unpack_words(w):
    return pltpu.bitcast(w & jnp.uint32(0xFFFF0000), F32), pltpu.bitcast(w << 16, F32)


def _pack_rows(val, ref, n_rows):
    half = val.shape[1] // 2
    wr = half // LANES
    bits = pltpu.bitcast(val.astype(BF16).astype(F32), jnp.uint32)
    word = (bits[:, :half] & jnp.uint32(0xFFFF0000)) | (bits[:, half:] >> 16)
    for c in range(wr):
        ref[pl.ds(c, n_rows, stride=wr), :] = word[:, c * LANES:(c + 1) * LANES]


def _unpack_slab(w):
    hi = pltpu.bitcast(w & jnp.uint32(0xFFFF0000), F32)
    lo = pltpu.bitcast(w << 16, F32)
    return hi, lo


def _ada_kernel(c_ref, w_ref, b_ref, t_ref, o_ref):
    c = c_ref[...]
    a = c * jax.nn.sigmoid(c)
    shared = jnp.dot(a, w_ref[...], preferred_element_type=F32,
                     precision=lax.Precision.HIGHEST) + b_ref[...]
    for i in range(o_ref.shape[0]):
        o_ref[i] = shared + t_ref[i:i + 1, :]


def _ada(c, w_ada, b_ada, ada_table):
    bsz, d = c.shape
    depth = ada_table.shape[0]
    n = w_ada.shape[1]
    c8 = jnp.zeros((SUBLANES, d), F32).at[:bsz].set(c)
    tn = min(COL_TILE, n)
    out = pl.pallas_call(
        _ada_kernel,
        out_shape=jax.ShapeDtypeStruct((depth, SUBLANES, n), F32),
        grid=(n // tn,),
        in_specs=[pl.BlockSpec((SUBLANES, d), lambda j: (0, 0)),
                  pl.BlockSpec((d, tn), lambda j: (0, j)),
                  pl.BlockSpec((1, tn), lambda j: (0, j)),
                  pl.BlockSpec((depth, tn), lambda j: (0, j))],
        out_specs=pl.BlockSpec((depth, SUBLANES, tn), lambda j: (0, 0, j)),
        compiler_params=_params("arbitrary"),
        name="ada_mod",
    )(c8, w_ada, b_ada.reshape(1, n), ada_table.reshape(depth, n))
    return out[:, :bsz].reshape(depth, bsz, N_MOD, d)


def _norm_mod_kernel(x_ref, g_ref, mod_ref, o_ref):
    m = mod_ref[0]
    o_ref[...] = _rms_mod(x_ref[...], g_ref[...], m[1:2], m[0:1]).astype(o_ref.dtype)


def _norm_mod(x2, gain, mod, seq):
    n, d = x2.shape
    tm = min(ROW_TILE, seq)
    tpb = seq // tm
    return pl.pallas_call(
        _norm_mod_kernel,
        out_shape=jax.ShapeDtypeStruct((n, d), BF16),
        grid=(n // tm,),
        in_specs=[pl.BlockSpec((tm, d), lambda i: (i, 0)),
                  pl.BlockSpec((1, d), lambda i: (0, 0)),
                  pl.BlockSpec((1, N_MOD, d), lambda i: (i // tpb, 0, 0))],
        out_specs=pl.BlockSpec((tm, d), lambda i: (i, 0)),
        compiler_params=_params("arbitrary"),
        name="norm_mod",
    )(x2, gain.reshape(1, d), mod)


def _causal_taps(val, buf, taps, halo, first):
    tm = val.shape[0]
    k_taps = taps.shape[0]

    @pl.when(first)
    def _():
        buf[0:halo] = jnp.zeros((halo, val.shape[1]), F32)

    @pl.when(jnp.logical_not(first))
    def _():
        buf[0:halo] = buf[tm:tm + halo]

    buf[halo:halo + tm] = val
    acc = taps[k_taps - 1:k_taps] * val
    for k in range(k_taps - 1):
        off = halo - (k_taps - 1) + k
        acc = acc + taps[k:k + 1] * buf[off:off + tm]
    return acc


def _causal_taps_wide(val, buf, shifted, taps_ref, bias, o_ref, halo, first):
    tm, tc = val.shape
    k_taps = taps_ref.shape[0]
    span = halo + tm - SUBLANES

    @pl.when(first)
    def _():
        buf[0:halo] = jnp.zeros((halo, tc), F32)

    @pl.when(jnp.logical_not(first))
    def _():
        buf[0:halo] = buf[tm:tm + halo]

    buf[halo:halo + tm] = val
    for r in range(1, SUBLANES):
        shifted[r - 1, 0:span] = buf[r:r + span]

    def chunk(ci, carry):
        r0 = pl.multiple_of(ci * CONV_ROWS, CONV_ROWS)
        acc = jnp.broadcast_to(bias, (CONV_ROWS, tc))
        for k in range(k_taps):
            off = halo - (k_taps - 1) + k
            r, base = off % SUBLANES, off - off % SUBLANES
            src = buf if r == 0 else shifted.at[r - 1]
            acc = acc + taps_ref[k:k + 1, :] * src[pl.ds(base + r0, CONV_ROWS), :]
        o_ref[pl.ds(r0, CONV_ROWS), :] = acc
        return carry
    lax.fori_loop(0, tm // CONV_ROWS, chunk, 0)


def _sc_in_kernel(h_ref, wb_ref, wc_ref, wu_ref, cw_ref, o_ref, buf, *, tpb, halo):
    i = pl.program_id(1)
    h = h_ref[...]
    b_gate = _dot(h, wb_ref[...])
    v = _dot(h, wc_ref[...]) * _dot(h, wu_ref[...])
    conv = _causal_taps(v, buf, cw_ref[...], halo, (i % tpb) == 0)
    o_ref[...] = (b_gate * conv).astype(o_ref.dtype)


def _sc_in(h, w_in, w_conv, seq):
    n, d = h.shape
    tm = min(MM_TILE, seq)
    tc = min(COL_TILE, d)
    nj = d // tc
    k_taps = w_conv.shape[0]
    halo = SUBLANES
    return pl.pallas_call(
        functools.partial(_sc_in_kernel, tpb=seq // tm, halo=halo),
        out_shape=jax.ShapeDtypeStruct((n, d), BF16),
        grid=(nj, n // tm),
        in_specs=[pl.BlockSpec((tm, d), lambda j, i: (i, 0)),
                  pl.BlockSpec((d, tc), lambda j, i: (0, j)),
                  pl.BlockSpec((d, tc), lambda j, i: (0, nj + j)),
                  pl.BlockSpec((d, tc), lambda j, i: (0, 2 * nj + j)),
                  pl.BlockSpec((k_taps, tc), lambda j, i: (0, j))],
        out_specs=pl.BlockSpec((tm, tc), lambda j, i: (i, j)),
        scratch_shapes=[pltpu.VMEM((halo + tm, tc), F32)],
        compiler_params=_params("arbitrary", "arbitrary"),
        name="short_conv_in",
    )(h, w_in, w_in, w_in, w_conv)


def _conf_in_kernel(h_ref, wa_ref, wg_ref, ba_ref, bg_ref, dw_ref, bdw_ref, o_ref, buf, shifted, *, tpb, halo):
    i = pl.program_id(1)
    h = h_ref[...]
    a = _dot(h, wa_ref[...]) + ba_ref[...]
    g = _dot(h, wg_ref[...]) + bg_ref[...]
    u = a * jax.nn.sigmoid(g)
    _causal_taps_wide(u, buf, shifted, dw_ref, bdw_ref[...], o_ref, halo, (i % tpb) == 0)


def _conf_in(h, w_pw1, b_pw1, w_dw, b_dw, seq):
    n, d = h.shape
    tm = min(MM_TILE, seq)
    tc = min(COL_TILE, d)
    nj = d // tc
    k_taps = w_dw.shape[0]
    halo = -(-(k_taps - 1) // SUBLANES) * SUBLANES
    b1 = b_pw1.reshape(1, 2 * d)
    return pl.pallas_call(
        functools.partial(_conf_in_kernel, tpb=seq // tm, halo=halo),
        out_shape=jax.ShapeDtypeStruct((n, d), F32),
        grid=(nj, n // tm),
        in_specs=[pl.BlockSpec((tm, d), lambda j, i: (i, 0)),
                  pl.BlockSpec((d, tc), lambda j, i: (0, j)),
                  pl.BlockSpec((d, tc), lambda j, i: (0, nj + j)),
                  pl.BlockSpec((1, tc), lambda j, i: (0, j)),
                  pl.BlockSpec((1, tc), lambda j, i: (0, nj + j)),
                  pl.BlockSpec((k_taps, tc), lambda j, i: (0, j)),
                  pl.BlockSpec((1, tc), lambda j, i: (0, j))],
        out_specs=pl.BlockSpec((tm, tc), lambda j, i: (i, j)),
        scratch_shapes=[pltpu.VMEM((halo + tm, tc), F32),
                        pltpu.VMEM((SUBLANES - 1, halo + tm - SUBLANES, tc), F32)],
        compiler_params=_params("arbitrary", "arbitrary"),
        name="conformer_in",
    )(h, w_pw1, w_pw1, b1, b1, w_dw, b_dw.reshape(1, d))


def _proj_res_kernel(*refs, has_bias, has_ln):
    a_ref, w_ref, x_ref, mod_ref = refs[:4]
    rest = list(refs[4:])
    b_ref = rest.pop(0) if has_bias else None
    lng_ref, lnb_ref = (rest.pop(0), rest.pop(0)) if has_ln else (None, None)
    o_ref = rest.pop(0)
    a = a_ref[...]
    if has_ln:
        mu = jnp.mean(a, axis=-1, keepdims=True)
        ac = a - mu
        var = jnp.mean(ac * ac, axis=-1, keepdims=True)
        a = ac * lax.rsqrt(var + EPS) * lng_ref[...] + lnb_ref[...]
        a = (a * jax.nn.sigmoid(a)).astype(BF16)
    y = _dot(a, w_ref[...])
    if has_bias:
        y = y + b_ref[...]
    o_ref[...] = x_ref[...] + mod_ref[0][2:3] * y


def _proj_res(a, w, x2, mod, seq, bias=None, ln=None):
    n, d = x2.shape
    k = a.shape[1]
    tm = min(ROW_TILE, seq)
    tpb = seq // tm
    row = lambda i: (i, 0)
    fixed = lambda i: (0, 0)
    args = [a, w, x2, mod]
    specs = [pl.BlockSpec((tm, k), row), pl.BlockSpec((k, d), fixed), pl.BlockSpec((tm, d), row),
             pl.BlockSpec((1, N_MOD, d), lambda i: (i // tpb, 0, 0))]
    if bias is not None:
        args.append(bias.reshape(1, d))
        specs.append(pl.BlockSpec((1, d), fixed))
    if ln is not None:
        args += [ln[0].reshape(1, k), ln[1].reshape(1, k)]
        specs += [pl.BlockSpec((1, k), fixed), pl.BlockSpec((1, k), fixed)]
    return pl.pallas_call(
        functools.partial(_proj_res_kernel, has_bias=bias is not None, has_ln=ln is not None),
        out_shape=jax.ShapeDtypeStruct((n, d), F32),
        grid=(n // tm,),
        in_specs=specs,
        out_specs=pl.BlockSpec((tm, d), row),
        compiler_params=_params("arbitrary"),
        name="proj_residual",
    )(*args)


def _pool_kernel(x_ref, halo_ref, gain_ref, mod_ref, w_ref, ps_ref, o_ref, buf, *, tpb):
    i = pl.program_id(0)
    tm = x_ref.shape[0]
    grp = w_ref.shape[1]
    m = mod_ref[0]
    gain = gain_ref[...]
    x = x_ref[...]
    h = _rms_mod(x, gain, m[1:2], m[0:1])
    hh = _rms_mod(halo_ref[...], gain, m[1:2], m[0:1])
    tile_in_seq = i % tpb
    buf[0:POOL_HALO] = jnp.where(tile_in_seq == 0, 0.0, hh)
    buf[POOL_HALO:POOL_HALO + tm] = h
    pos = tile_in_seq * tm + lax.broadcasted_iota(jnp.int32, (tm, 1), 0)
    for g, win in enumerate(POOL_WINDOWS):
        cs = slice(g * grp, (g + 1) * grp)
        hg = h[:, cs]
        acc = hg
        for j in range(1, win):
            acc = acc + buf[POOL_HALO - j:POOL_HALO - j + tm, cs]
        cnt = jnp.minimum(pos + 1, win).astype(F32)
        pooled = (acc / cnt - hg).astype(BF16)
        y = _dot(pooled, w_ref[g]) * ps_ref[:, cs]
        o_ref[:, cs] = x[:, cs] + m[2:3, cs] * y


def _pool_mixer(x2, gain, mod, pool_w, pool_scale, seq):
    n, d = x2.shape
    tm = min(ROW_TILE, seq)
    tpb = seq // tm
    hb = tm // POOL_HALO
    ng, grp = pool_w.shape[0], pool_w.shape[1]
    return pl.pallas_call(
        functools.partial(_pool_kernel, tpb=tpb),
        out_shape=jax.ShapeDtypeStruct((n, d), F32),
        grid=(n // tm,),
        in_specs=[pl.BlockSpec((tm, d), lambda i: (i, 0)),
                  pl.BlockSpec((POOL_HALO, d), lambda i: (jnp.maximum(i * hb - 1, 0), 0)),
                  pl.BlockSpec((1, d), lambda i: (0, 0)),
                  pl.BlockSpec((1, N_MOD, d), lambda i: (i // tpb, 0, 0)),
                  pl.BlockSpec((ng, grp, grp), lambda i: (0, 0, 0)),
                  pl.BlockSpec((1, d), lambda i: (0, 0))],
        out_specs=pl.BlockSpec((tm, d), lambda i: (i, 0)),
        scratch_shapes=[pltpu.VMEM((POOL_HALO + tm, d), F32)],
        compiler_params=_params("arbitrary"),
        name="pool_mixer",
    )(x2, x2, gain.reshape(1, d), mod, pool_w, pool_scale.reshape(1, d))


def _qkv_kernel(h_ref, w_ref, gains_ref, o_ref, *, nper):
    j = pl.program_id(1)
    part = j // nper
    tm = h_ref.shape[0]
    heads = o_ref.shape[1] // HEAD_DIM
    is_v = part == 2
    gain = jnp.where(part == 0, gains_ref[0:1, :] * (HEAD_DIM ** -0.5), gains_ref[1:2, :])
    rows = tm // 2
    for r in range(2):
        y = _dot(h_ref[r * rows:(r + 1) * rows, :], w_ref[...])
        outs = []
        for hh in range(heads):
            yh = y[:, hh * HEAD_DIM:(hh + 1) * HEAD_DIM]
            ms = jnp.mean(yh * yh, axis=-1, keepdims=True)
            outs.append(jnp.where(is_v, yh, yh * lax.rsqrt(ms + EPS) * gain))
        o_ref[r * rows:(r + 1) * rows, :] = jnp.concatenate(outs, axis=1).astype(o_ref.dtype)


def _qkv(h, w_qkv, q_gain, k_gain, seq):
    n, d = h.shape
    tm = min(MM_TILE, seq)
    tn = min(COL_TILE, d)
    gains = jnp.zeros((SUBLANES, HEAD_DIM), F32).at[0].set(q_gain).at[1].set(k_gain)
    return pl.pallas_call(
        functools.partial(_qkv_kernel, nper=d // tn),
        out_shape=jax.ShapeDtypeStruct((n, 3 * d), BF16),
        grid=(n // tm, 3 * d // tn),
        in_specs=[pl.BlockSpec((tm, d), lambda i, j: (i, 0)),
                  pl.BlockSpec((d, tn), lambda i, j: (0, j)),
                  pl.BlockSpec((SUBLANES, HEAD_DIM), lambda i, j: (0, 0))],
        out_specs=pl.BlockSpec((tm, tn), lambda i, j: (i, j)),
        compiler_params=_params("arbitrary", "arbitrary"),
        name="qkv_proj",
    )(h, w_qkv, gains)


def _attn_kernel(q_ref, k_ref, v_ref, o_ref):
    qi = pl.program_id(2)
    tq = q_ref.shape[0]
    tk = min(ATTN_KEYS, k_ref.shape[0])
    heads = q_ref.shape[1] // HEAD_DIM
    diag = (qi * tq) // tk
    suffix = (lax.broadcasted_iota(jnp.int32, (tk, tk), 0)
              >= lax.broadcasted_iota(jnp.int32, (tk, tk), 1)).astype(BF16)
    valid = (diag * tk + lax.broadcasted_iota(jnp.int32, (tq, tk), 1)
             < qi * tq + lax.broadcasted_iota(jnp.int32, (tq, tk), 0))
    qs = [q_ref[:, hh * HEAD_DIM:(hh + 1) * HEAD_DIM] for hh in range(heads)]

    def blocks(kb, laters, masked):
        start = pl.multiple_of(kb * tk, tk)
        hs = range(heads)
        cols = [slice(hh * HEAD_DIM, (hh + 1) * HEAD_DIM) for hh in hs]
        zs = [lax.dot_general(qs[hh], k_ref[pl.ds(start, tk), cols[hh]], (((1,), (1,)), ((), ())),
                              preferred_element_type=F32) for hh in hs]
        csums = []
        for z in zs:
            neg_abs = pltpu.bitcast(pltpu.bitcast(z, jnp.uint32) | jnp.uint32(0x80000000), F32)
            sp = jnp.maximum(z, 0.0) + jnp.log(1.0 + jnp.exp(neg_abs))
            if masked:
                sp = jnp.where(valid, sp, 0.0)
            csums.append(_dot(sp.astype(BF16), suffix))
        outs = []
        for hh in hs:
            a = jnp.exp(zs[hh] - (csums[hh] + laters[hh]))
            if masked:
                a = jnp.where(valid, a, 0.0)
            outs.append(_dot(a.astype(BF16), v_ref[pl.ds(start, tk), cols[hh]]))
        return outs, [laters[hh] + csums[hh][:, 0:1] for hh in hs]

    zero = jnp.zeros((tq, 1), F32)
    accs, laters = blocks(diag, [zero] * heads, True)

    def step(kb, carry):
        accs, laters = carry
        outs, laters = blocks(kb, laters, False)
        return [acc + out for acc, out in zip(accs, outs)], laters

    def group(it, carry):
        for u in range(ATTN_UNROLL):
            carry = step(diag - 1 - ATTN_UNROLL * it - u, carry)
        return carry

    n_groups = diag // ATTN_UNROLL
    carry = lax.fori_loop(0, n_groups, group, (accs, laters))
    first_left = diag - ATTN_UNROLL * n_groups
    accs, laters = lax.fori_loop(0, first_left, lambda it, c: step(first_left - 1 - it, c), carry)
    o_ref[...] = jnp.concatenate(accs, axis=1).astype(o_ref.dtype)


def _attention(qkv, bsz, seq):
    n, d3 = qkv.shape
    d = d3 // 3
    hw = ATTN_HEADS * HEAD_DIM
    ng = d // hw
    tq = min(ATTN_TILE, seq)
    nq = seq // tq
    return pl.pallas_call(
        _attn_kernel,
        out_shape=jax.ShapeDtypeStruct((n, d), BF16),
        grid=(bsz, ng, nq),
        in_specs=[pl.BlockSpec((tq, hw), lambda b, h, i: (b * nq + i, h)),
                  pl.BlockSpec((seq, hw), lambda b, h, i: (b, ng + h)),
                  pl.BlockSpec((seq, hw), lambda b, h, i: (b, 2 * ng + h))],
        out_specs=pl.BlockSpec((tq, hw), lambda b, h, i: (b * nq + i, h)),
        compiler_params=_params("arbitrary", "arbitrary", "arbitrary"),
        name="stick_breaking_attn",
    )(qkv, qkv, qkv)


def _moe_pre_kernel(x_ref, gain_ref, mod_ref, rw_ref, rb_ref,
                    words_ref, idx_ref, gate_ref, rank_ref, cnt_ref, run):
    i = pl.program_id(0)
    tm = x_ref.shape[0]

    @pl.when(i == 0)
    def _():
        run[...] = jnp.zeros_like(run)

    m = mod_ref[0]
    h = _rms_mod(x_ref[...], gain_ref[...], m[4:5], m[3:4])
    _pack_rows(h, words_ref, tm)

    logits = jnp.dot(h, rw_ref[...], preferred_element_type=F32,
                     precision=lax.Precision.HIGHEST) + rb_ref[...]
    lane = lax.broadcasted_iota(jnp.int32, (tm, LANES), 1)
    onehots, vals, idxs = [], [], []
    cur = logits
    for _ in range(TOP_K):
        mx = jnp.max(cur, axis=-1, keepdims=True)
        ix = jnp.min(jnp.where(cur == mx, lane, LANES), axis=-1, keepdims=True)
        oh = lane == ix
        onehots.append(oh)
        vals.append(mx)
        idxs.append(ix)
        cur = jnp.where(oh, -jnp.inf, cur)
    exps = [jnp.exp(v - vals[0]) for v in vals]
    den = exps[0]
    for e in exps[1:]:
        den = den + e

    ohs = jnp.concatenate([oh.astype(BF16) for oh in onehots], axis=1)
    r_i = lax.broadcasted_iota(jnp.int32, (tm, tm), 0)
    c_i = lax.broadcasted_iota(jnp.int32, (tm, tm), 1)
    earlier = _dot((c_i < r_i).astype(BF16), ohs)
    base = run[...]
    idx_out = jnp.zeros((tm, LANES), jnp.int32)
    rank_out = jnp.zeros((tm, LANES), jnp.int32)
    gate_out = jnp.zeros((tm, LANES), F32)
    for k in range(TOP_K):
        ohf = onehots[k].astype(F32)
        rk = jnp.sum(ohf * (earlier[:, k * LANES:(k + 1) * LANES] + base), axis=-1, keepdims=True)
        base = base + jnp.sum(ohf, axis=0, keepdims=True)
        idx_out = jnp.where(lane == k, idxs[k], idx_out)
        rank_out = jnp.where(lane == k, rk.astype(jnp.int32), rank_out)
        gate_out = jnp.where(lane == k, exps[k] / den, gate_out)
    run[...] = base
    cnt_ref[...] = base
    idx_ref[...] = idx_out
    rank_ref[...] = rank_out
    gate_ref[...] = gate_out


def _moe_pre(x1, gain, mod, router_w, router_b, seq):
    n, d = x1.shape
    n_exp = router_w.shape[1]
    tm = min(ROW_TILE, seq)
    tpb = seq // tm
    wr = d // (2 * LANES)
    rw = jnp.zeros((d, LANES), F32).at[:, :n_exp].set(router_w)
    rb = jnp.full((1, LANES), NEG_BIG, F32).at[0, :n_exp].set(router_b)
    row = lambda i: (i, 0)
    fixed = lambda i: (0, 0)
    return pl.pallas_call(
        _moe_pre_kernel,
        out_shape=(jax.ShapeDtypeStruct((n * wr, LANES), jnp.uint32),
                   jax.ShapeDtypeStruct((n, LANES), jnp.int32),
                   jax.ShapeDtypeStruct((n, LANES), F32),
                   jax.ShapeDtypeStruct((n, LANES), jnp.int32),
                   jax.ShapeDtypeStruct((1, LANES), F32)),
        grid=(n // tm,),
        in_specs=[pl.BlockSpec((tm, d), row),
                  pl.BlockSpec((1, d), fixed),
                  pl.BlockSpec((1, N_MOD, d), lambda i: (i // tpb, 0, 0)),
                  pl.BlockSpec((d, LANES), fixed),
                  pl.BlockSpec((1, LANES), fixed)],
        out_specs=(pl.BlockSpec((tm * wr, LANES), row),
                   pl.BlockSpec((tm, LANES), row),
                   pl.BlockSpec((tm, LANES), row),
                   pl.BlockSpec((tm, LANES), row),
                   pl.BlockSpec((1, LANES), fixed)),
        scratch_shapes=[pltpu.VMEM((1, LANES), F32)],
        compiler_params=_params("arbitrary"),
        name="moe_route",
    )(x1, gain.reshape(1, d), mod, rw, rb)


def _row(ref, r):
    return ref.at[pl.ds(r, 1), :]


def _rows(ref, row, wr):
    return ref.at[pl.ds(pl.multiple_of(row * wr, wr), wr), :]


def _scatter_kernel(fill_lo_ref, fill_hi_ref, dest_ref, src_ref, dst, zbuf, sem, zsem, *, wr):
    i = pl.program_id(0)
    n = dest_ref.shape[-1]

    @pl.when(i == 0)
    def _():
        zbuf[...] = jnp.zeros_like(zbuf)

        def per_range(e, count):
            def per_row(r, c):
                pltpu.make_async_copy(zbuf, _rows(dst, r, wr), zsem).start()
                return c + 1
            return lax.fori_loop(fill_lo_ref[e], fill_hi_ref[e], per_row, count)

        total = lax.fori_loop(0, fill_lo_ref.shape[0], per_range, 0)

        def drain(r, c):
            pltpu.make_async_copy(zbuf, _rows(dst, 0, wr), zsem).wait()
            return c
        lax.fori_loop(0, total, drain, 0)

    def issue(a, c):
        token = lax.shift_right_logical(a, TOP_K.bit_length() - 1)
        pltpu.make_async_copy(_rows(src_ref, token, wr), _rows(dst, dest_ref[0, 0, a], wr), sem).start()
        return c
    lax.fori_loop(0, n, issue, 0, unroll=8)

    def drain(a, c):
        pltpu.make_async_copy(_rows(src_ref, 0, wr), _rows(dst, 0, wr), sem).wait()
        return c
    lax.fori_loop(0, n, drain, 0, unroll=8)


def _scatter_rows(words, dest3, fill_lo, fill_hi, n_rows, wr):
    steps, _, per = dest3.shape
    return pl.pallas_call(
        functools.partial(_scatter_kernel, wr=wr),
        out_shape=jax.ShapeDtypeStruct((n_rows * wr, LANES), jnp.uint32),
        grid_spec=pltpu.PrefetchScalarGridSpec(
            num_scalar_prefetch=2,
            grid=(steps,),
            in_specs=[pl.BlockSpec((1, 1, per), lambda i, lo, hi: (i, 0, 0), memory_space=pltpu.SMEM),
                      pl.BlockSpec((per // TOP_K * wr, LANES), lambda i, lo, hi: (i, 0))],
            out_specs=pl.BlockSpec(memory_space=pl.ANY),
            scratch_shapes=[pltpu.VMEM((wr, LANES), jnp.uint32),
                            pltpu.SemaphoreType.DMA(()), pltpu.SemaphoreType.DMA(())]),
        compiler_params=pltpu.CompilerParams(dimension_semantics=("arbitrary",), has_side_effects=True,
                                             disable_bounds_checks=True),
        name="moe_scatter_rows",
    )(fill_lo, fill_hi, dest3, words)


def _expert_ffn_kernel(te_ref, first_ref, slot_ref, next_ref, nreal_ref,
                       rows_ref, bgu_ref, bd_ref, wgu_hbm, wd_hbm, out_ref,
                       wbuf, stage, sems, done, *, wr, n_gu, n_down):
    t = pl.program_id(0)
    tm = rows_ref.shape[0] // wr
    half = wr * LANES
    n_chunks = n_gu + n_down
    ch = stage.shape[1]
    rows_gu = n_gu * ch
    d_exp = wd_hbm.shape[1] * n_down

    def start(expert, c):
        @pl.when(c < n_gu)
        def _():
            pltpu.make_async_copy(wgu_hbm.at[expert * n_gu + c], stage.at[c % 2], sems.at[c % 2]).start()

        @pl.when(c >= n_gu)
        def _():
            pltpu.make_async_copy(wd_hbm.at[expert * n_down + (c - n_gu)], stage.at[c % 2], sems.at[c % 2]).start()

    def advance(expert, slot, upto):
        def body(c, carry):
            pltpu.make_async_copy(wgu_hbm.at[0], stage.at[c % 2], sems.at[c % 2]).wait()
            wbuf[slot, pl.ds(pl.multiple_of(c * ch, ch), ch), :] = stage[c % 2].astype(BF16)

            @pl.when(c + 2 < n_chunks)
            def _():
                start(expert, c + 2)
            return carry
        lax.fori_loop(done[0], upto, body, 0)
        done[0] = jnp.maximum(done[0], upto)

    @pl.when(t < nreal_ref[0])
    def _():
        expert, slot, nxt = te_ref[t], slot_ref[t], next_ref[t]

        @pl.when(first_ref[t] == 1)
        def _():
            @pl.when(t == 0)
            def _():
                done[0] = 0
                start(expert, 0)
                start(expert, 1)

            advance(expert, slot, n_chunks)
            done[0] = 0

            @pl.when(nxt >= 0)
            def _():
                start(nxt, 0)
                start(nxt, 1)

        his, los = [], []
        for c in range(wr):
            hi, lo = _unpack_slab(rows_ref[pl.ds(c, tm, stride=wr), :])
            his.append(hi.astype(BF16))
            los.append(lo.astype(BF16))
        xa = jnp.concatenate(his, axis=1)
        xb = jnp.concatenate(los, axis=1)
        gu = _dot(xa, wbuf[slot, 0:half, :]) + _dot(xb, wbuf[slot, half:2 * half, :]) + bgu_ref[0]
        g = jnp.minimum(gu[:, :d_exp], SWIGLU_LIMIT)
        u = jnp.clip(gu[:, d_exp:], -SWIGLU_LIMIT, SWIGLU_LIMIT)
        act = (g * jax.nn.sigmoid(SWIGLU_ALPHA * g) * (u + 1.0)).astype(BF16)
        out_ref[...] = _pack_words(_dot(act, wbuf[slot, rows_gu:rows_gu + d_exp, :]) + bd_ref[0])

        @pl.when(nxt >= 0)
        def _():
            advance(nxt, 1 - slot, jnp.minimum(done[0] + 2, n_chunks))

    @pl.when(t >= nreal_ref[0])
    def _():
        out_ref[...] = jnp.zeros_like(out_ref)


def _expert_ffn(rows, w_gu, b_gu, w_down, b_down, tile_e, is_first, tile_slot, tile_next, n_real, wr):
    n_exp, d, de2 = w_gu.shape
    de = w_down.shape[1]
    assert de2 == 2 * de and w_down.shape[2] == d and de2 == d, "gate/up and down chunks share one staging shape"
    ch = min(WEIGHT_CHUNK_ROWS, de)
    n_gu, n_down = d // ch, de // ch
    tm = EXPERT_TILE
    n_tiles = rows.shape[0] // (tm * wr)
    half = wr * LANES

    def tile(t, nr):
        return jnp.minimum(t, nr[0] - 1)

    return pl.pallas_call(
        functools.partial(_expert_ffn_kernel, wr=wr, n_gu=n_gu, n_down=n_down),
        out_shape=jax.ShapeDtypeStruct((n_tiles * tm, half), jnp.uint32),
        grid_spec=pltpu.PrefetchScalarGridSpec(
            num_scalar_prefetch=5,
            grid=(n_tiles,),
            in_specs=[pl.BlockSpec((tm * wr, LANES), lambda t, te, fi, sl, nx, nr: (tile(t, nr), 0)),
                      pl.BlockSpec((1, 1, de2), lambda t, te, fi, sl, nx, nr: (te[tile(t, nr)], 0, 0)),
                      pl.BlockSpec((1, 1, d), lambda t, te, fi, sl, nx, nr: (te[tile(t, nr)], 0, 0)),
                      pl.BlockSpec(memory_space=pl.ANY),
                      pl.BlockSpec(memory_space=pl.ANY)],
            out_specs=pl.BlockSpec((tm, half), lambda t, te, fi, sl, nx, nr: (t, 0)),
            scratch_shapes=[pltpu.VMEM((2, d + de, d), BF16),
                            pltpu.VMEM((2, ch, d), F32),
                            pltpu.SemaphoreType.DMA((2,)),
                            pltpu.SMEM((1,), jnp.int32)]),
        compiler_params=_params("arbitrary"),
        name="expert_ffn",
    )(tile_e, is_first, tile_slot, tile_next, n_real, rows,
      b_gu.reshape(n_exp, 1, de2), b_down.reshape(n_exp, 1, d),
      w_gu.reshape(n_exp * n_gu, ch, de2), w_down.reshape(n_exp * n_down, ch, d))


def _combine_kernel(*refs, emit_h):
    dest_ref, dest_next_ref, gate_ref, x_ref, mod_ref = refs[:5]
    if emit_h:
        gain_ref, modn_ref, src, xo_ref, ho_ref, ybuf, sems = refs[5:]
    else:
        src, xo_ref, ybuf, sems = refs[5:]
    i = pl.program_id(0)
    tm = x_ref.shape[0]
    n = tm * TOP_K
    half = ybuf.shape[2]
    slot = i % 2

    def issue(idx_ref, to_slot):
        def body(jj, c):
            base = pl.multiple_of(jj * SUBLANES, SUBLANES)
            for r in range(SUBLANES):
                pltpu.make_async_copy(_row(src, idx_ref[0, 0, base + r]),
                                      ybuf.at[to_slot, pl.ds(base, SUBLANES), :].at[pl.ds(r, 1), :],
                                      sems.at[to_slot]).start()
            return c
        lax.fori_loop(0, n // SUBLANES, body, 0)

    def drain(from_slot):
        def body(j, c):
            pltpu.make_async_copy(_row(src, 0), _row(ybuf.at[from_slot], 0), sems.at[from_slot]).wait()
            return c
        lax.fori_loop(0, n, body, 0, unroll=8)

    @pl.when(i == 0)
    def _():
        issue(dest_ref, 0)

    for s_now in range(2):
        @pl.when(slot == s_now)
        def _():
            @pl.when(i + 1 < pl.num_programs(0))
            def _():
                issue(dest_next_ref, 1 - s_now)
            drain(s_now)

    m = mod_ref[0]
    gate = gate_ref[...]
    acc_hi = jnp.zeros((tm, half), F32)
    acc_lo = jnp.zeros((tm, half), F32)
    for k in range(TOP_K):
        hi, lo = _unpack_words(ybuf[slot, k * tm:(k + 1) * tm, :])
        acc_hi = acc_hi + gate[:, k:k + 1] * hi
        acc_lo = acc_lo + gate[:, k:k + 1] * lo
    xo_ref[:, :half] = x_ref[:, :half] + m[5:6, :half] * acc_hi
    xo_ref[:, half:] = x_ref[:, half:] + m[5:6, half:] * acc_lo
    if emit_h:
        mn = modn_ref[0]
        ho_ref[...] = _rms_mod(xo_ref[...], gain_ref[...], mn[1:2], mn[0:1]).astype(ho_ref.dtype)


def _combine(out_words, dest3, gate, x1, mod, seq, next_gain=None, next_mod=None):
    n, d = x1.shape
    tm = min(ROW_TILE, seq)
    tpb = seq // tm
    steps = n // tm
    per = tm * TOP_K
    half = out_words.shape[1]
    assert dest3.shape == (steps, 1, per)
    emit_h = next_gain is not None
    row = lambda i: (i, 0)
    batch = lambda i: (i // tpb, 0, 0)
    args = [dest3, dest3, gate, x1, mod]
    specs = [pl.BlockSpec((1, 1, per), lambda i: (i, 0, 0), memory_space=pltpu.SMEM),
             pl.BlockSpec((1, 1, per), lambda i: (jnp.minimum(i + 1, steps - 1), 0, 0), memory_space=pltpu.SMEM),
             pl.BlockSpec((tm, LANES), row), pl.BlockSpec((tm, d), row), pl.BlockSpec((1, N_MOD, d), batch)]
    out_shape = [jax.ShapeDtypeStruct((n, d), F32)]
    out_specs = [pl.BlockSpec((tm, d), row)]
    if emit_h:
        args += [next_gain.reshape(1, d), next_mod]
        specs += [pl.BlockSpec((1, d), lambda i: (0, 0)), pl.BlockSpec((1, N_MOD, d), batch)]
        out_shape.append(jax.ShapeDtypeStruct((n, d), BF16))
        out_specs.append(pl.BlockSpec((tm, d), row))
    args.append(out_words)
    specs.append(pl.BlockSpec(memory_space=pl.ANY))
    outs = pl.pallas_call(
        functools.partial(_combine_kernel, emit_h=emit_h),
        out_shape=tuple(out_shape),
        grid=(steps,),
        in_specs=specs,
        out_specs=tuple(out_specs),
        scratch_shapes=[pltpu.VMEM((2, per, half), jnp.uint32), pltpu.SemaphoreType.DMA((2,))],
        compiler_params=pltpu.CompilerParams(dimension_semantics=("arbitrary",), vmem_limit_bytes=VMEM_LIMIT_BYTES,
                                             disable_bounds_checks=True),
        name="moe_combine",
    )(*args)
    return (outs[0], outs[1]) if emit_h else (outs[0], None)


def _routed_experts(x1, gain, mod, router_w, router_b, layer, w_gu, b_gu, w_down, b_down, seq,
                    next_gain, next_mod):
    n, d = x1.shape
    n_exp = router_w.shape[1]
    wr = d // (2 * LANES)
    words, idx, gate, rank, counts = _moe_pre(x1, gain, mod, router_w, router_b, seq)

    tm = EXPERT_TILE
    n_tiles = (n * TOP_K) // tm + n_exp
    cnt = counts[0, :n_exp].astype(jnp.int32)
    padded = (cnt + tm - 1) // tm * tm
    pad_end = jnp.cumsum(padded)
    pad_start = pad_end - padded
    onehot = idx[:, :TOP_K, None] == jnp.arange(n_exp, dtype=jnp.int32)
    dest = jnp.sum(jnp.where(onehot, pad_start, 0), axis=-1) + rank[:, :TOP_K]
    per = min(ROW_TILE, seq) * TOP_K
    dest3 = dest.reshape(n * TOP_K // per, 1, per)
    tile_start = jnp.arange(n_tiles, dtype=jnp.int32) * tm
    n_real = (pad_end[-1] // tm).astype(jnp.int32).reshape(1)
    tile_e = jnp.minimum(jnp.sum((tile_start[:, None] >= pad_end[None, :]).astype(jnp.int32), axis=1), n_exp - 1)
    is_first = (jnp.any(tile_start[:, None] == jnp.where(padded > 0, pad_start, -1)[None, :], axis=1)
                ).astype(jnp.int32)
    ids = jnp.arange(n_exp, dtype=jnp.int32)
    used = padded > 0
    group_slot = (jnp.cumsum(used.astype(jnp.int32)) - 1) % 2
    later_used = jnp.where(used[None, :] & (ids[None, :] > ids[:, None]), ids[None, :], n_exp)
    next_used = jnp.min(later_used, axis=1)
    group_next = jnp.where(next_used < n_exp, next_used + layer * n_exp, -1)
    tile_slot = group_slot[tile_e].astype(jnp.int32)
    tile_next = group_next[tile_e].astype(jnp.int32)
    tile_e = tile_e + layer * n_exp

    fill_lo = jnp.concatenate([pad_start + cnt, pad_end[-1:]])
    fill_hi = jnp.concatenate([pad_end, jnp.full((1,), n_tiles * tm, jnp.int32)])
    rows = _scatter_rows(words, dest3, fill_lo, fill_hi, n_tiles * tm, wr)
    out_words = _expert_ffn(rows, w_gu, b_gu, w_down, b_down, tile_e, is_first, tile_slot, tile_next, n_real, wr)
    dest_km = dest.reshape(-1, per // TOP_K, TOP_K).transpose(0, 2, 1).reshape(dest3.shape)
    return _combine(out_words, dest_km, gate, x1, mod, seq, next_gain, next_mod)


def kernel(x, c, w_ada, b_ada, ada_table, norm_mix, norm_ffn, sc_w_in, sc_w_conv, sc_w_out, pool_w, pool_scale, cf_w_pw1, cf_b_pw1, cf_w_dw, cf_b_dw, cf_ln_g, cf_ln_b, cf_w_pw2, cf_b_pw2, sb_w_qkv, sb_q_gain, sb_k_gain, sb_w_o, router_w, router_b, exp_w_gu, exp_b_gu, exp_w_down, exp_b_down):
    bsz, seq, d = x.shape
    depth = norm_mix.shape[0]
    n_mixers = len(POOL_WINDOWS)
    x2 = x.reshape(bsz * seq, d)
    mod_all = _ada(c, w_ada, b_ada, ada_table)
    n_exp = router_w.shape[-1]
    w_gu = exp_w_gu.reshape((depth * n_exp,) + exp_w_gu.shape[2:])
    b_gu = exp_b_gu.reshape((depth * n_exp,) + exp_b_gu.shape[2:])
    w_down = exp_w_down.reshape((depth * n_exp,) + exp_w_down.shape[2:])
    b_down = exp_b_down.reshape((depth * n_exp,) + exp_b_down.shape[2:])
    h = None
    for i in range(depth):
        kind, occ = i % n_mixers, i // n_mixers
        mod = mod_all[i]
        if kind != 1 and h is None:
            h = _norm_mod(x2, norm_mix[i], mod, seq)
        if kind == 0:
            g = _sc_in(h, sc_w_in[occ].astype(BF16), sc_w_conv[occ], seq)
            x1 = _proj_res(g, sc_w_out[occ].astype(BF16), x2, mod, seq)
        elif kind == 1:
            x1 = _pool_mixer(x2, norm_mix[i], mod, pool_w[occ].astype(BF16), pool_scale[occ], seq)
        elif kind == 2:
            u = _conf_in(h, cf_w_pw1[occ].astype(BF16), cf_b_pw1[occ], cf_w_dw[occ], cf_b_dw[occ], seq)
            x1 = _proj_res(u, cf_w_pw2[occ].astype(BF16), x2, mod, seq, bias=cf_b_pw2[occ],
                           ln=(cf_ln_g[occ], cf_ln_b[occ]))
        else:
            qkv = _qkv(h, sb_w_qkv[occ].astype(BF16), sb_q_gain[occ], sb_k_gain[occ], seq)
            o = _attention(qkv, bsz, seq)
            x1 = _proj_res(o, sb_w_o[occ].astype(BF16), x2, mod, seq)
        nxt = i + 1
        wants_h = nxt < depth and nxt % n_mixers != 1
        x2, h = _routed_experts(
            x1, norm_ffn[i], mod, router_w[i], router_b[i], i, w_gu, b_gu, w_down, b_down, seq,
            norm_mix[nxt] if wants_h else None, mod_all[nxt] if wants_h else None)
    return x2.reshape(bsz, seq, d)
```

```python
import functools

import jax
import jax.numpy as jnp
from jax import lax
from jax.experimental import pallas as pl
from jax.experimental.pallas import tpu as pltpu

EPS = 1e-6
LANES = 128
SUBLANES = 8
HEAD_DIM = 128
POOL_WINDOWS = (2, 4, 8, 16)
POOL_HALO = 16
TOP_K = 4
SWIGLU_LIMIT = 7.0
SWIGLU_ALPHA = 1.702
N_MOD = 6
VMEM_LIMIT_BYTES = 52 * 1024 * 1024
NEG_BIG = -1e30
CONV_ROWS = 32
ROW_TILE = 256
MM_TILE = 512
COL_TILE = 512
WEIGHT_CHUNK_ROWS = 512
EXPERT_TILE = 256
ATTN_TILE = 256
ATTN_KEYS = 256
ATTN_HEADS = 4
ATTN_UNROLL = 4

F32 = jnp.float32
BF16 = jnp.bfloat16


def _params(*sem):
    return pltpu.CompilerParams(dimension_semantics=sem, vmem_limit_bytes=VMEM_LIMIT_BYTES)


def _rms_mod(x, gain, scale, shift):
    ms = jnp.mean(x * x, axis=-1, keepdims=True)
    return x * lax.rsqrt(ms + EPS) * gain * (1.0 + scale) + shift


def _dot(a, b):
    return jnp.dot(a, b, preferred_element_type=F32)


def _pack_words(val):
    half = val.shape[1] // 2
    bits = pltpu.bitcast(val.astype(BF16).astype(F32), jnp.uint32)
    return (bits[:, :half] & jnp.uint32(0xFFFF0000)) | (bits[:, half:] >> 16)


def _unpack_words(w):
    return pltpu.bitcast(w & jnp.uint32(0xFFFF0000), F32), pltpu.bitcast(w << 16, F32)


def _pack_rows(val, ref, n_rows):
    half = val.shape[1] // 2
    wr = half // LANES
    bits = pltpu.bitcast(val.astype(BF16).astype(F32), jnp.uint32)
    word = (bits[:, :half] & jnp.uint32(0xFFFF0000)) | (bits[:, half:] >> 16)
    for c in range(wr):
        ref[pl.ds(c, n_rows, stride=wr), :] = word[:, c * LANES:(c + 1) * LANES]


def _unpack_slab(w):
    hi = pltpu.bitcast(w & jnp.uint32(0xFFFF0000), F32)
    lo = pltpu.bitcast(w << 16, F32)
    return hi, lo


def _ada_kernel(c_ref, w_ref, b_ref, t_ref, o_ref):
    c = c_ref[...]
    a = c * jax.nn.sigmoid(c)
    shared = jnp.dot(a, w_ref[...], preferred_element_type=F32,
                     precision=lax.Precision.HIGHEST) + b_ref[...]
    for i in range(o_ref.shape[0]):
        o_ref[i] = shared + t_ref[i:i + 1, :]


def _ada(c, w_ada, b_ada, ada_table):
    bsz, d = c.shape
    depth = ada_table.shape[0]
    n = w_ada.shape[1]
    c8 = jnp.zeros((SUBLANES, d), F32).at[:bsz].set(c)
    tn = min(COL_TILE, n)
    out = pl.pallas_call(
        _ada_kernel,
        out_shape=jax.ShapeDtypeStruct((depth, SUBLANES, n), F32),
        grid=(n // tn,),
        in_specs=[pl.BlockSpec((SUBLANES, d), lambda j: (0, 0)),
                  pl.BlockSpec((d, tn), lambda j: (0, j)),
                  pl.BlockSpec((1, tn), lambda j: (0, j)),
                  pl.BlockSpec((depth, tn), lambda j: (0, j))],
        out_specs=pl.BlockSpec((depth, SUBLANES, tn), lambda j: (0, 0, j)),
        compiler_params=_params("arbitrary"),
        name="ada_mod",
    )(c8, w_ada, b_ada.reshape(1, n), ada_table.reshape(depth, n))
    return out[:, :bsz].reshape(depth, bsz, N_MOD, d)


def _norm_mod_kernel(x_ref, g_ref, mod_ref, o_ref):
    m = mod_ref[0]
    o_ref[...] = _rms_mod(x_ref[...], g_ref[...], m[1:2], m[0:1]).astype(o_ref.dtype)


def _norm_mod(x2, gain, mod, seq):
    n, d = x2.shape
    tm = min(ROW_TILE, seq)
    tpb = seq // tm
    return pl.pallas_call(
        _norm_mod_kernel,
        out_shape=jax.ShapeDtypeStruct((n, d), BF16),
        grid=(n // tm,),
        in_specs=[pl.BlockSpec((tm, d), lambda i: (i, 0)),
                  pl.BlockSpec((1, d), lambda i: (0, 0)),
                  pl.BlockSpec((1, N_MOD, d), lambda i: (i // tpb, 0, 0))],
        out_specs=pl.BlockSpec((tm, d), lambda i: (i, 0)),
        compiler_params=_params("arbitrary"),
        name="norm_mod",
    )(x2, gain.reshape(1, d), mod)


def _causal_taps(val, buf, taps, halo, first):
    tm = val.shape[0]
    k_taps = taps.shape[0]

    @pl.when(first)
    def _():
        buf[0:halo] = jnp.zeros((halo, val.shape[1]), F32)

    @pl.when(jnp.logical_not(first))
    def _():
        buf[0:halo] = buf[tm:tm + halo]

    buf[halo:halo + tm] = val
    acc = taps[k_taps - 1:k_taps] * val
    for k in range(k_taps - 1):
        off = halo - (k_taps - 1) + k
        acc = acc + taps[k:k + 1] * buf[off:off + tm]
    return acc


def _causal_taps_wide(val, buf, shifted, taps_ref, bias, o_ref, halo, first):
    tm, tc = val.shape
    k_taps = taps_ref.shape[0]
    span = halo + tm - SUBLANES

    @pl.when(first)
    def _():
        buf[0:halo] = jnp.zeros((halo, tc), F32)

    @pl.when(jnp.logical_not(first))
    def _():
        buf[0:halo] = buf[tm:tm + halo]

    buf[halo:halo + tm] = val
    for r in range(1, SUBLANES):
        shifted[r - 1, 0:span] = buf[r:r + span]

    def chunk(ci, carry):
        r0 = pl.multiple_of(ci * CONV_ROWS, CONV_ROWS)
        acc = jnp.broadcast_to(bias, (CONV_ROWS, tc))
        for k in range(k_taps):
            off = halo - (k_taps - 1) + k
            r, base = off % SUBLANES, off - off % SUBLANES
            src = buf if r == 0 else shifted.at[r - 1]
            acc = acc + taps_ref[k:k + 1, :] * src[pl.ds(base + r0, CONV_ROWS), :]
        o_ref[pl.ds(r0, CONV_ROWS), :] = acc
        return carry
    lax.fori_loop(0, tm // CONV_ROWS, chunk, 0)


def _sc_in_kernel(h_ref, wb_ref, wc_ref, wu_ref, cw_ref, o_ref, buf, *, tpb, halo):
    i = pl.program_id(1)
    h = h_ref[...]
    b_gate = _dot(h, wb_ref[...])
    v = _dot(h, wc_ref[...]) * _dot(h, wu_ref[...])
    conv = _causal_taps(v, buf, cw_ref[...], halo, (i % tpb) == 0)
    o_ref[...] = (b_gate * conv).astype(o_ref.dtype)


def _sc_in(h, w_in, w_conv, seq):
    n, d = h.shape
    tm = min(MM_TILE, seq)
    tc = min(COL_TILE, d)
    nj = d // tc
    k_taps = w_conv.shape[0]
    halo = SUBLANES
    return pl.pallas_call(
        functools.partial(_sc_in_kernel, tpb=seq // tm, halo=halo),
        out_shape=jax.ShapeDtypeStruct((n, d), BF16),
        grid=(nj, n // tm),
        in_specs=[pl.BlockSpec((tm, d), lambda j, i: (i, 0)),
                  pl.BlockSpec((d, tc), lambda j, i: (0, j)),
                  pl.BlockSpec((d, tc), lambda j, i: (0, nj + j)),
                  pl.BlockSpec((d, tc), lambda j, i: (0, 2 * nj + j)),
                  pl.BlockSpec((k_taps, tc), lambda j, i: (0, j))],
        out_specs=pl.BlockSpec((tm, tc), lambda j, i: (i, j)),
        scratch_shapes=[pltpu.VMEM((halo + tm, tc), F32)],
        compiler_params=_params("arbitrary", "arbitrary"),
        name="short_conv_in",
    )(h, w_in, w_in, w_in, w_conv)


def _conf_in_kernel(h_ref, wa_ref, wg_ref, ba_ref, bg_ref, dw_ref, bdw_ref, o_ref, buf, shifted, *, tpb, halo):
    i = pl.program_id(1)
    h = h_ref[...]
    a = _dot(h, wa_ref[...]) + ba_ref[...]
    g = _dot(h, wg_ref[...]) + bg_ref[...]
    u = a * jax.nn.sigmoid(g)
    _causal_taps_wide(u, buf, shifted, dw_ref, bdw_ref[...], o_ref, halo, (i % tpb) == 0)


def _conf_in(h, w_pw1, b_pw1, w_dw, b_dw, seq):
    n, d = h.shape
    tm = min(MM_TILE, seq)
    tc = min(COL_TILE, d)
    nj = d // tc
    k_taps = w_dw.shape[0]
    halo = -(-(k_taps - 1) // SUBLANES) * SUBLANES
    b1 = b_pw1.reshape(1, 2 * d)
    return pl.pallas_call(
        functools.partial(_conf_in_kernel, tpb=seq // tm, halo=halo),
        out_shape=jax.ShapeDtypeStruct((n, d), F32),
        grid=(nj, n // tm),
        in_specs=[pl.BlockSpec((tm, d), lambda j, i: (i, 0)),
                  pl.BlockSpec((d, tc), lambda j, i: (0, j)),
                  pl.BlockSpec((d, tc), lambda j, i: (0, nj + j)),
                  pl.BlockSpec((1, tc), lambda j, i: (0, j)),
                  pl.BlockSpec((1, tc), lambda j, i: (0, nj + j)),
                  pl.BlockSpec((k_taps, tc), lambda j, i: (0, j)),
                  pl.BlockSpec((1, tc), lambda j, i: (0, j))],
        out_specs=pl.BlockSpec((tm, tc), lambda j, i: (i, j)),
        scratch_shapes=[pltpu.VMEM((halo + tm, tc), F32),
                        pltpu.VMEM((SUBLANES - 1, halo + tm - SUBLANES, tc), F32)],
        compiler_params=_params("arbitrary", "arbitrary"),
        name="conformer_in",
    )(h, w_pw1, w_pw1, b1, b1, w_dw, b_dw.reshape(1, d))


def _proj_res_kernel(*refs, has_bias, has_ln):
    a_ref, w_ref, x_ref, mod_ref = refs[:4]
    rest = list(refs[4:])
    b_ref = rest.pop(0) if has_bias else None
    lng_ref, lnb_ref = (rest.pop(0), rest.pop(0)) if has_ln else (None, None)
    o_ref = rest.pop(0)
    a = a_ref[...]
    if has_ln:
        mu = jnp.mean(a, axis=-1, keepdims=True)
        ac = a - mu
        var = jnp.mean(ac * ac, axis=-1, keepdims=True)
        a = ac * lax.rsqrt(var + EPS) * lng_ref[...] + lnb_ref[...]
        a = (a * jax.nn.sigmoid(a)).astype(BF16)
    y = _dot(a, w_ref[...])
    if has_bias:
        y = y + b_ref[...]
    o_ref[...] = x_ref[...] + mod_ref[0][2:3] * y


def _proj_res(a, w, x2, mod, seq, bias=None, ln=None):
    n, d = x2.shape
    k = a.shape[1]
    tm = min(ROW_TILE, seq)
    tpb = seq // tm
    row = lambda i: (i, 0)
    fixed = lambda i: (0, 0)
    args = [a, w, x2, mod]
    specs = [pl.BlockSpec((tm, k), row), pl.BlockSpec((k, d), fixed), pl.BlockSpec((tm, d), row),
             pl.BlockSpec((1, N_MOD, d), lambda i: (i // tpb, 0, 0))]
    if bias is not None:
        args.append(bias.reshape(1, d))
        specs.append(pl.BlockSpec((1, d), fixed))
    if ln is not None:
        args += [ln[0].reshape(1, k), ln[1].reshape(1, k)]
        specs += [pl.BlockSpec((1, k), fixed), pl.BlockSpec((1, k), fixed)]
    return pl.pallas_call(
        functools.partial(_proj_res_kernel, has_bias=bias is not None, has_ln=ln is not None),
        out_shape=jax.ShapeDtypeStruct((n, d), F32),
        grid=(n // tm,),
        in_specs=specs,
        out_specs=pl.BlockSpec((tm, d), row),
        compiler_params=_params("arbitrary"),
        name="proj_residual",
    )(*args)


def _pool_kernel(x_ref, halo_ref, gain_ref, mod_ref, w_ref, ps_ref, o_ref, buf, *, tpb):
    i = pl.program_id(0)
    tm = x_ref.shape[0]
    grp = w_ref.shape[1]
    m = mod_ref[0]
    gain = gain_ref[...]
    x = x_ref[...]
    h = _rms_mod(x, gain, m[1:2], m[0:1])
    hh = _rms_mod(halo_ref[...], gain, m[1:2], m[0:1])
    tile_in_seq = i % tpb
    buf[0:POOL_HALO] = jnp.where(tile_in_seq == 0, 0.0, hh)
    buf[POOL_HALO:POOL_HALO + tm] = h
    pos = tile_in_seq * tm + lax.broadcasted_iota(jnp.int32, (tm, 1), 0)
    for g, win in enumerate(POOL_WINDOWS):
        cs = slice(g * grp, (g + 1) * grp)
        hg = h[:, cs]
        acc = hg
        for j in range(1, win):
            acc = acc + buf[POOL_HALO - j:POOL_HALO - j + tm, cs]
        cnt = jnp.minimum(pos + 1, win).astype(F32)
        pooled = (acc / cnt - hg).astype(BF16)
        y = _dot(pooled, w_ref[g]) * ps_ref[:, cs]
        o_ref[:, cs] = x[:, cs] + m[2:3, cs] * y


def _pool_mixer(x2, gain, mod, pool_w, pool_scale, seq):
    n, d = x2.shape
    tm = min(ROW_TILE, seq)
    tpb = seq // tm
    hb = tm // POOL_HALO
    ng, grp = pool_w.shape[0], pool_w.shape[1]
    return pl.pallas_call(
        functools.partial(_pool_kernel, tpb=tpb),
        out_shape=jax.ShapeDtypeStruct((n, d), F32),
        grid=(n // tm,),
        in_specs=[pl.BlockSpec((tm, d), lambda i: (i, 0)),
                  pl.BlockSpec((POOL_HALO, d), lambda i: (jnp.maximum(i * hb - 1, 0), 0)),
                  pl.BlockSpec((1, d), lambda i: (0, 0)),
                  pl.BlockSpec((1, N_MOD, d), lambda i: (i // tpb, 0, 0)),
                  pl.BlockSpec((ng, grp, grp), lambda i: (0, 0, 0)),
                  pl.BlockSpec((1, d), lambda i: (0, 0))],
        out_specs=pl.BlockSpec((tm, d), lambda i: (i, 0)),
        scratch_shapes=[pltpu.VMEM((POOL_HALO + tm, d), F32)],
        compiler_params=_params("arbitrary"),
        name="pool_mixer",
    )(x2, x2, gain.reshape(1, d), mod, pool_w, pool_scale.reshape(1, d))


def _qkv_kernel(h_ref, w_ref, gains_ref, o_ref, *, nper):
    j = pl.program_id(1)
    part = j // nper
    tm = h_ref.shape[0]
    heads = o_ref.shape[1] // HEAD_DIM
    is_v = part == 2
    gain = jnp.where(part == 0, gains_ref[0:1, :] * (HEAD_DIM ** -0.5), gains_ref[1:2, :])
    rows = tm // 2
    for r in range(2):
        y = _dot(h_ref[r * rows:(r + 1) * rows, :], w_ref[...])
        outs = []
        for hh in range(heads):
            yh = y[:, hh * HEAD_DIM:(hh + 1) * HEAD_DIM]
            ms = jnp.mean(yh * yh, axis=-1, keepdims=True)
            outs.append(jnp.where(is_v, yh, yh * lax.rsqrt(ms + EPS) * gain))
        o_ref[r * rows:(r + 1) * rows, :] = jnp.concatenate(outs, axis=1).astype(o_ref.dtype)


def _qkv(h, w_qkv, q_gain, k_gain, seq):
    n, d = h.shape
    tm = min(MM_TILE, seq)
    tn = min(COL_TILE, d)
    gains = jnp.zeros((SUBLANES, HEAD_DIM), F32).at[0].set(q_gain).at[1].set(k_gain)
    return pl.pallas_call(
        functools.partial(_qkv_kernel, nper=d // tn),
        out_shape=jax.ShapeDtypeStruct((n, 3 * d), BF16),
        grid=(n // tm, 3 * d // tn),
        in_specs=[pl.BlockSpec((tm, d), lambda i, j: (i, 0)),
                  pl.BlockSpec((d, tn), lambda i, j: (0, j)),
                  pl.BlockSpec((SUBLANES, HEAD_DIM), lambda i, j: (0, 0))],
        out_specs=pl.BlockSpec((tm, tn), lambda i, j: (i, j)),
        compiler_params=_params("arbitrary", "arbitrary"),
        name="qkv_proj",
    )(h, w_qkv, gains)


def _attn_kernel(q_ref, k_ref, v_ref, o_ref):
    qi = pl.program_id(2)
    tq = q_ref.shape[0]
    tk = min(ATTN_KEYS, k_ref.shape[0])
    heads = q_ref.shape[1] // HEAD_DIM
    diag = (qi * tq) // tk
    suffix = (lax.broadcasted_iota(jnp.int32, (tk, tk), 0)
              >= lax.broadcasted_iota(jnp.int32, (tk, tk), 1)).astype(BF16)
    valid = (diag * tk + lax.broadcasted_iota(jnp.int32, (tq, tk), 1)
             < qi * tq + lax.broadcasted_iota(jnp.int32, (tq, tk), 0))
    qs = [q_ref[:, hh * HEAD_DIM:(hh + 1) * HEAD_DIM] for hh in range(heads)]

    def blocks(kb, laters, masked):
        start = pl.multiple_of(kb * tk, tk)
        hs = range(heads)
        cols = [slice(hh * HEAD_DIM, (hh + 1) * HEAD_DIM) for hh in hs]
        zs = [lax.dot_general(qs[hh], k_ref[pl.ds(start, tk), cols[hh]], (((1,), (1,)), ((), ())),
                              preferred_element_type=F32) for hh in hs]
        csums = []
        for z in zs:
            neg_abs = pltpu.bitcast(pltpu.bitcast(z, jnp.uint32) | jnp.uint32(0x80000000), F32)
            sp = jnp.maximum(z, 0.0) + jnp.log(1.0 + jnp.exp(neg_abs))
            if masked:
                sp = jnp.where(valid, sp, 0.0)
            csums.append(_dot(sp.astype(BF16), suffix))
        outs = []
        for hh in hs:
            a = jnp.exp(zs[hh] - (csums[hh] + laters[hh]))
            if masked:
                a = jnp.where(valid, a, 0.0)
            outs.append(_dot(a.astype(BF16), v_ref[pl.ds(start, tk), cols[hh]]))
        return outs, [laters[hh] + csums[hh][:, 0:1] for hh in hs]

    zero = jnp.zeros((tq, 1), F32)
    accs, laters = blocks(diag, [zero] * heads, True)

    def step(kb, carry):
        accs, laters = carry
        outs, laters = blocks(kb, laters, False)
        return [acc + out for acc, out in zip(accs, outs)], laters

    def group(it, carry):
        for u in range(ATTN_UNROLL):
            carry = step(diag - 1 - ATTN_UNROLL * it - u, carry)
        return carry

    n_groups = diag // ATTN_UNROLL
    carry = lax.fori_loop(0, n_groups, group, (accs, laters))
    first_left = diag - ATTN_UNROLL * n_groups
    accs, laters = lax.fori_loop(0, first_left, lambda it, c: step(first_left - 1 - it, c), carry)
    o_ref[...] = jnp.concatenate(accs, axis=1).astype(o_ref.dtype)


def _attention(qkv, bsz, seq):
    n, d3 = qkv.shape
    d = d3 // 3
    hw = ATTN_HEADS * HEAD_DIM
    ng = d // hw
    tq = min(ATTN_TILE, seq)
    nq = seq // tq
    return pl.pallas_call(
        _attn_kernel,
        out_shape=jax.ShapeDtypeStruct((n, d), BF16),
        grid=(bsz, ng, nq),
        in_specs=[pl.BlockSpec((tq, hw), lambda b, h, i: (b * nq + i, h)),
                  pl.BlockSpec((seq, hw), lambda b, h, i: (b, ng + h)),
                  pl.BlockSpec((seq, hw), lambda b, h, i: (b, 2 * ng + h))],
        out_specs=pl.BlockSpec((tq, hw), lambda b, h, i: (b * nq + i, h)),
        compiler_params=_params("arbitrary", "arbitrary", "arbitrary"),
        name="stick_breaking_attn",
    )(qkv, qkv, qkv)


def _moe_pre_kernel(x_ref, gain_ref, mod_ref, rwh_ref, rwl_ref, rb_ref,
                    words_ref, idx_ref, gate_ref, rank_ref, cnt_ref, run):
    i = pl.program_id(0)
    tm = x_ref.shape[0]

    @pl.when(i == 0)
    def _():
        run[...] = jnp.zeros_like(run)

    m = mod_ref[0]
    h = _rms_mod(x_ref[...], gain_ref[...], m[4:5], m[3:4])
    _pack_rows(h, words_ref, tm)

    h_hi = h.astype(BF16)
    h_lo = (h - h_hi.astype(F32)).astype(BF16)
    logits = (_dot(h_hi, rwh_ref[...]) + _dot(h_lo, rwh_ref[...]) + _dot(h_hi, rwl_ref[...])) + rb_ref[...]
    lane = lax.broadcasted_iota(jnp.int32, (tm, LANES), 1)
    lane_f = lane.astype(F32)
    onehots, vals, idxs = [], [], []
    cur = logits
    for _ in range(TOP_K):
        mx = jnp.max(cur, axis=-1, keepdims=True)
        ix_f = jnp.min(jnp.where(cur == mx, lane_f, float(LANES)), axis=-1, keepdims=True)
        ix = ix_f.astype(jnp.int32)
        oh = lane_f == ix_f
        onehots.append(oh)
        vals.append(mx)
        idxs.append(ix)
        cur = jnp.where(oh, -jnp.inf, cur)
    exps = [jnp.exp(v - vals[0]) for v in vals]
    den = exps[0]
    for e in exps[1:]:
        den = den + e

    ohs = jnp.concatenate([oh.astype(BF16) for oh in onehots], axis=1)
    r_i = lax.broadcasted_iota(jnp.int32, (tm, tm), 0)
    c_i = lax.broadcasted_iota(jnp.int32, (tm, tm), 1)
    earlier = _dot((c_i < r_i).astype(BF16), ohs)
    base = run[...]
    idx_out = jnp.zeros((tm, LANES), jnp.int32)
    rank_out = jnp.zeros((tm, LANES), jnp.int32)
    gate_out = jnp.zeros((tm, LANES), F32)
    for k in range(TOP_K):
        ohf = onehots[k].astype(F32)
        rk = jnp.sum(ohf * (earlier[:, k * LANES:(k + 1) * LANES] + base), axis=-1, keepdims=True)
        base = base + jnp.sum(ohf, axis=0, keepdims=True)
        idx_out = jnp.where(lane == k, idxs[k], idx_out)
        rank_out = jnp.where(lane == k, rk.astype(jnp.int32), rank_out)
        gate_out = jnp.where(lane == k, exps[k] / den, gate_out)
    run[...] = base
    cnt_ref[...] = base
    idx_ref[...] = idx_out
    rank_ref[...] = rank_out
    gate_ref[...] = gate_out


def _moe_pre(x1, gain, mod, router_w, router_b, seq):
    n, d = x1.shape
    n_exp = router_w.shape[1]
    tm = min(ROW_TILE, seq)
    tpb = seq // tm
    wr = d // (2 * LANES)
    rw = jnp.zeros((d, LANES), F32).at[:, :n_exp].set(router_w)
    rw_hi = rw.astype(BF16)
    rw_lo = (rw - rw_hi.astype(F32)).astype(BF16)
    rb = jnp.full((1, LANES), NEG_BIG, F32).at[0, :n_exp].set(router_b)
    row = lambda i: (i, 0)
    fixed = lambda i: (0, 0)
    return pl.pallas_call(
        _moe_pre_kernel,
        out_shape=(jax.ShapeDtypeStruct((n * wr, LANES), jnp.uint32),
                   jax.ShapeDtypeStruct((n, LANES), jnp.int32),
                   jax.ShapeDtypeStruct((n, LANES), F32),
                   jax.ShapeDtypeStruct((n, LANES), jnp.int32),
                   jax.ShapeDtypeStruct((1, LANES), F32)),
        grid=(n // tm,),
        in_specs=[pl.BlockSpec((tm, d), row),
                  pl.BlockSpec((1, d), fixed),
                  pl.BlockSpec((1, N_MOD, d), lambda i: (i // tpb, 0, 0)),
                  pl.BlockSpec((d, LANES), fixed),
                  pl.BlockSpec((d, LANES), fixed),
                  pl.BlockSpec((1, LANES), fixed)],
        out_specs=(pl.BlockSpec((tm * wr, LANES), row),
                   pl.BlockSpec((tm, LANES), row),
                   pl.BlockSpec((tm, LANES), row),
                   pl.BlockSpec((tm, LANES), row),
                   pl.BlockSpec((1, LANES), fixed)),
        scratch_shapes=[pltpu.VMEM((1, LANES), F32)],
        compiler_params=_params("arbitrary"),
        name="moe_route",
    )(x1, gain.reshape(1, d), mod, rw_hi, rw_lo, rb)


def _row(ref, r):
    return ref.at[pl.ds(r, 1), :]


def _rows(ref, row, wr):
    return ref.at[pl.ds(pl.multiple_of(row * wr, wr), wr), :]


def _scatter_kernel(fill_lo_ref, fill_hi_ref, dest_ref, src_ref, dst, zbuf, sem, zsem, *, wr):
    i = pl.program_id(0)
    n = dest_ref.shape[-1]

    @pl.when(i == 0)
    def _():
        zbuf[...] = jnp.zeros_like(zbuf)

        def per_range(e, count):
            def per_row(r, c):
                pltpu.make_async_copy(zbuf, _rows(dst, r, wr), zsem).start()
                return c + 1
            return lax.fori_loop(fill_lo_ref[e], fill_hi_ref[e], per_row, count)

        total = lax.fori_loop(0, fill_lo_ref.shape[0], per_range, 0)

        def drain(r, c):
            pltpu.make_async_copy(zbuf, _rows(dst, 0, wr), zsem).wait()
            return c
        lax.fori_loop(0, total, drain, 0)

    def issue(a, c):
        token = lax.shift_right_logical(a, TOP_K.bit_length() - 1)
        pltpu.make_async_copy(_rows(src_ref, token, wr), _rows(dst, dest_ref[0, 0, a], wr), sem).start()
        return c
    lax.fori_loop(0, n, issue, 0, unroll=8)

    def drain(a, c):
        pltpu.make_async_copy(_rows(src_ref, 0, wr), _rows(dst, 0, wr), sem).wait()
        return c
    lax.fori_loop(0, n, drain, 0, unroll=8)


def _scatter_rows(words, dest3, fill_lo, fill_hi, n_rows, wr):
    steps, _, per = dest3.shape
    return pl.pallas_call(
        functools.partial(_scatter_kernel, wr=wr),
        out_shape=jax.ShapeDtypeStruct((n_rows * wr, LANES), jnp.uint32),
        grid_spec=pltpu.PrefetchScalarGridSpec(
            num_scalar_prefetch=2,
            grid=(steps,),
            in_specs=[pl.BlockSpec((1, 1, per), lambda i, lo, hi: (i, 0, 0), memory_space=pltpu.SMEM),
                      pl.BlockSpec((per // TOP_K * wr, LANES), lambda i, lo, hi: (i, 0))],
            out_specs=pl.BlockSpec(memory_space=pl.ANY),
            scratch_shapes=[pltpu.VMEM((wr, LANES), jnp.uint32),
                            pltpu.SemaphoreType.DMA(()), pltpu.SemaphoreType.DMA(())]),
        compiler_params=pltpu.CompilerParams(dimension_semantics=("arbitrary",), has_side_effects=True,
                                             disable_bounds_checks=True),
        name="moe_scatter_rows",
    )(fill_lo, fill_hi, dest3, words)


def _expert_ffn_kernel(te_ref, first_ref, slot_ref, next_ref, nreal_ref,
                       rows_ref, bgu_ref, bd_ref, wgu_hbm, wd_hbm, out_ref,
                       wbuf, stage, sems, done, *, wr, n_gu, n_down):
    t = pl.program_id(0)
    tm = rows_ref.shape[0] // wr
    half = wr * LANES
    n_chunks = n_gu + n_down
    ch = stage.shape[1]
    rows_gu = n_gu * ch
    d_exp = wd_hbm.shape[1] * n_down

    def start(expert, c):
        @pl.when(c < n_gu)
        def _():
            pltpu.make_async_copy(wgu_hbm.at[expert * n_gu + c], stage.at[c % 2], sems.at[c % 2]).start()

        @pl.when(c >= n_gu)
        def _():
            pltpu.make_async_copy(wd_hbm.at[expert * n_down + (c - n_gu)], stage.at[c % 2], sems.at[c % 2]).start()

    def advance(expert, slot, upto):
        def body(c, carry):
            pltpu.make_async_copy(wgu_hbm.at[0], stage.at[c % 2], sems.at[c % 2]).wait()
            wbuf[slot, pl.ds(pl.multiple_of(c * ch, ch), ch), :] = stage[c % 2].astype(BF16)

            @pl.when(c + 2 < n_chunks)
            def _():
                start(expert, c + 2)
            return carry
        lax.fori_loop(done[0], upto, body, 0)
        done[0] = jnp.maximum(done[0], upto)

    @pl.when(t < nreal_ref[0])
    def _():
        expert, slot, nxt = te_ref[t], slot_ref[t], next_ref[t]

        @pl.when(first_ref[t] == 1)
        def _():
            @pl.when(t == 0)
            def _():
                done[0] = 0
                start(expert, 0)
                start(expert, 1)

            advance(expert, slot, n_chunks)
            done[0] = 0

            @pl.when(nxt >= 0)
            def _():
                start(nxt, 0)
                start(nxt, 1)

        his, los = [], []
        for c in range(wr):
            hi, lo = _unpack_slab(rows_ref[pl.ds(c, tm, stride=wr), :])
            his.append(hi.astype(BF16))
            los.append(lo.astype(BF16))
        xa = jnp.concatenate(his, axis=1)
        xb = jnp.concatenate(los, axis=1)
        gu = _dot(xa, wbuf[slot, 0:half, :]) + _dot(xb, wbuf[slot, half:2 * half, :]) + bgu_ref[0]
        g = jnp.minimum(gu[:, :d_exp], SWIGLU_LIMIT)
        u = jnp.clip(gu[:, d_exp:], -SWIGLU_LIMIT, SWIGLU_LIMIT)
        act = (g * jax.nn.sigmoid(SWIGLU_ALPHA * g) * (u + 1.0)).astype(BF16)
        out_ref[...] = _pack_words(_dot(act, wbuf[slot, rows_gu:rows_gu + d_exp, :]) + bd_ref[0])

        @pl.when(nxt >= 0)
        def _():
            advance(nxt, 1 - slot, jnp.minimum(done[0] + 2, n_chunks))

    @pl.when(t >= nreal_ref[0])
    def _():
        out_ref[...] = jnp.zeros_like(out_ref)


def _expert_ffn(rows, w_gu, b_gu, w_down, b_down, tile_e, is_first, tile_slot, tile_next, n_real, wr):
    n_exp, d, de2 = w_gu.shape
    de = w_down.shape[1]
    assert de2 == 2 * de and w_down.shape[2] == d and de2 == d, "gate/up and down chunks share one staging shape"
    ch = min(WEIGHT_CHUNK_ROWS, de)
    n_gu, n_down = d // ch, de // ch
    tm = EXPERT_TILE
    n_tiles = rows.shape[0] // (tm * wr)
    half = wr * LANES

    def tile(t, nr):
        return jnp.minimum(t, nr[0] - 1)

    return pl.pallas_call(
        functools.partial(_expert_ffn_kernel, wr=wr, n_gu=n_gu, n_down=n_down),
        out_shape=jax.ShapeDtypeStruct((n_tiles * tm, half), jnp.uint32),
        grid_spec=pltpu.PrefetchScalarGridSpec(
            num_scalar_prefetch=5,
            grid=(n_tiles,),
            in_specs=[pl.BlockSpec((tm * wr, LANES), lambda t, te, fi, sl, nx, nr: (tile(t, nr), 0)),
                      pl.BlockSpec((1, 1, de2), lambda t, te, fi, sl, nx, nr: (te[tile(t, nr)], 0, 0)),
                      pl.BlockSpec((1, 1, d), lambda t, te, fi, sl, nx, nr: (te[tile(t, nr)], 0, 0)),
                      pl.BlockSpec(memory_space=pl.ANY),
                      pl.BlockSpec(memory_space=pl.ANY)],
            out_specs=pl.BlockSpec((tm, half), lambda t, te, fi, sl, nx, nr: (t, 0)),
            scratch_shapes=[pltpu.VMEM((2, d + de, d), BF16),
                            pltpu.VMEM((2, ch, d), F32),
                            pltpu.SemaphoreType.DMA((2,)),
                            pltpu.SMEM((1,), jnp.int32)]),
        compiler_params=_params("arbitrary"),
        name="expert_ffn",
    )(tile_e, is_first, tile_slot, tile_next, n_real, rows,
      b_gu.reshape(n_exp, 1, de2), b_down.reshape(n_exp, 1, d),
      w_gu.reshape(n_exp * n_gu, ch, de2), w_down.reshape(n_exp * n_down, ch, d))


def _combine_kernel(*refs, emit_h):
    dest_ref, dest_next_ref, gate_ref, x_ref, mod_ref = refs[:5]
    if emit_h:
        gain_ref, modn_ref, src, xo_ref, ho_ref, ybuf, sems = refs[5:]
    else:
        src, xo_ref, ybuf, sems = refs[5:]
    i = pl.program_id(0)
    tm = x_ref.shape[0]
    n = tm * TOP_K
    half = ybuf.shape[2]
    slot = i % 2

    def issue(idx_ref, to_slot):
        def body(jj, c):
            base = pl.multiple_of(jj * SUBLANES, SUBLANES)
            for r in range(SUBLANES):
                pltpu.make_async_copy(_row(src, idx_ref[0, 0, base + r]),
                                      ybuf.at[to_slot, pl.ds(base, SUBLANES), :].at[pl.ds(r, 1), :],
                                      sems.at[to_slot]).start()
            return c
        lax.fori_loop(0, n // SUBLANES, body, 0)

    def drain(from_slot):
        def body(j, c):
            pltpu.make_async_copy(_row(src, 0), _row(ybuf.at[from_slot], 0), sems.at[from_slot]).wait()
            return c
        lax.fori_loop(0, n, body, 0, unroll=8)

    @pl.when(i == 0)
    def _():
        issue(dest_ref, 0)

    for s_now in range(2):
        @pl.when(slot == s_now)
        def _():
            @pl.when(i + 1 < pl.num_programs(0))
            def _():
                issue(dest_next_ref, 1 - s_now)
            drain(s_now)

    m = mod_ref[0]
    gate = gate_ref[...]
    acc_hi = jnp.zeros((tm, half), F32)
    acc_lo = jnp.zeros((tm, half), F32)
    for k in range(TOP_K):
        hi, lo = _unpack_words(ybuf[slot, k * tm:(k + 1) * tm, :])
        acc_hi = acc_hi + gate[:, k:k + 1] * hi
        acc_lo = acc_lo + gate[:, k:k + 1] * lo
    xo_ref[:, :half] = x_ref[:, :half] + m[5:6, :half] * acc_hi
    xo_ref[:, half:] = x_ref[:, half:] + m[5:6, half:] * acc_lo
    if emit_h:
        mn = modn_ref[0]
        ho_ref[...] = _rms_mod(xo_ref[...], gain_ref[...], mn[1:2], mn[0:1]).astype(ho_ref.dtype)


def _combine(out_words, dest3, gate, x1, mod, seq, next_gain=None, next_mod=None):
    n, d = x1.shape
    tm = min(ROW_TILE, seq)
    tpb = seq // tm
    steps = n // tm
    per = tm * TOP_K
    half = out_words.shape[1]
    assert dest3.shape == (steps, 1, per)
    emit_h = next_gain is not None
    row = lambda i: (i, 0)
    batch = lambda i: (i // tpb, 0, 0)
    args = [dest3, dest3, gate, x1, mod]
    specs = [pl.BlockSpec((1, 1, per), lambda i: (i, 0, 0), memory_space=pltpu.SMEM),
             pl.BlockSpec((1, 1, per), lambda i: (jnp.minimum(i + 1, steps - 1), 0, 0), memory_space=pltpu.SMEM),
             pl.BlockSpec((tm, LANES), row), pl.BlockSpec((tm, d), row), pl.BlockSpec((1, N_MOD, d), batch)]
    out_shape = [jax.ShapeDtypeStruct((n, d), F32)]
    out_specs = [pl.BlockSpec((tm, d), row)]
    if emit_h:
        args += [next_gain.reshape(1, d), next_mod]
        specs += [pl.BlockSpec((1, d), lambda i: (0, 0)), pl.BlockSpec((1, N_MOD, d), batch)]
        out_shape.append(jax.ShapeDtypeStruct((n, d), BF16))
        out_specs.append(pl.BlockSpec((tm, d), row))
    args.append(out_words)
    specs.append(pl.BlockSpec(memory_space=pl.ANY))
    outs = pl.pallas_call(
        functools.partial(_combine_kernel, emit_h=emit_h),
        out_shape=tuple(out_shape),
        grid=(steps,),
        in_specs=specs,
        out_specs=tuple(out_specs),
        scratch_shapes=[pltpu.VMEM((2, per, half), jnp.uint32), pltpu.SemaphoreType.DMA((2,))],
        compiler_params=pltpu.CompilerParams(dimension_semantics=("arbitrary",), vmem_limit_bytes=VMEM_LIMIT_BYTES,
                                             disable_bounds_checks=True),
        name="moe_combine",
    )(*args)
    return (outs[0], outs[1]) if emit_h else (outs[0], None)


def _routed_experts(x1, gain, mod, router_w, router_b, layer, w_gu, b_gu, w_down, b_down, seq,
                    next_gain, next_mod):
    n, d = x1.shape
    n_exp = router_w.shape[1]
    wr = d // (2 * LANES)
    words, idx, gate, rank, counts = _moe_pre(x1, gain, mod, router_w, router_b, seq)

    tm = EXPERT_TILE
    n_tiles = (n * TOP_K) // tm + n_exp
    cnt = counts[0, :n_exp].astype(jnp.int32)
    padded = (cnt + tm - 1) // tm * tm
    pad_end = jnp.cumsum(padded)
    pad_start = pad_end - padded
    onehot = idx[:, :TOP_K, None] == jnp.arange(n_exp, dtype=jnp.int32)
    dest = jnp.sum(jnp.where(onehot, pad_start, 0), axis=-1) + rank[:, :TOP_K]
    per = min(ROW_TILE, seq) * TOP_K
    dest3 = dest.reshape(n * TOP_K // per, 1, per)
    tile_start = jnp.arange(n_tiles, dtype=jnp.int32) * tm
    n_real = (pad_end[-1] // tm).astype(jnp.int32).reshape(1)
    tile_e = jnp.minimum(jnp.sum((tile_start[:, None] >= pad_end[None, :]).astype(jnp.int32), axis=1), n_exp - 1)
    is_first = (jnp.any(tile_start[:, None] == jnp.where(padded > 0, pad_start, -1)[None, :], axis=1)
                ).astype(jnp.int32)
    ids = jnp.arange(n_exp, dtype=jnp.int32)
    used = padded > 0
    group_slot = (jnp.cumsum(used.astype(jnp.int32)) - 1) % 2
    later_used = jnp.where(used[None, :] & (ids[None, :] > ids[:, None]), ids[None, :], n_exp)
    next_used = jnp.min(later_used, axis=1)
    group_next = jnp.where(next_used < n_exp, next_used + layer * n_exp, -1)
    tile_slot = group_slot[tile_e].astype(jnp.int32)
    tile_next = group_next[tile_e].astype(jnp.int32)
    tile_e = tile_e + layer * n_exp

    fill_lo = jnp.concatenate([pad_start + cnt, pad_end[-1:]])
    fill_hi = jnp.concatenate([pad_end, jnp.full((1,), n_tiles * tm, jnp.int32)])
    rows = _scatter_rows(words, dest3, fill_lo, fill_hi, n_tiles * tm, wr)
    out_words = _expert_ffn(rows, w_gu, b_gu, w_down, b_down, tile_e, is_first, tile_slot, tile_next, n_real, wr)
    dest_km = dest.reshape(-1, per // TOP_K, TOP_K).transpose(0, 2, 1).reshape(dest3.shape)
    return _combine(out_words, dest_km, gate, x1, mod, seq, next_gain, next_mod)


def kernel(x, c, w_ada, b_ada, ada_table, norm_mix, norm_ffn, sc_w_in, sc_w_conv, sc_w_out, pool_w, pool_scale, cf_w_pw1, cf_b_pw1, cf_w_dw, cf_b_dw, cf_ln_g, cf_ln_b, cf_w_pw2, cf_b_pw2, sb_w_qkv, sb_q_gain, sb_k_gain, sb_w_o, router_w, router_b, exp_w_gu, exp_b_gu, exp_w_down, exp_b_down):
    bsz, seq, d = x.shape
    depth = norm_mix.shape[0]
    n_mixers = len(POOL_WINDOWS)
    x2 = x.reshape(bsz * seq, d)
    mod_all = _ada(c, w_ada, b_ada, ada_table)
    n_exp = router_w.shape[-1]
    w_gu = exp_w_gu.reshape((depth * n_exp,) + exp_w_gu.shape[2:])
    b_gu = exp_b_gu.reshape((depth * n_exp,) + exp_b_gu.shape[2:])
    w_down = exp_w_down.reshape((depth * n_exp,) + exp_w_down.shape[2:])
    b_down = exp_b_down.reshape((depth * n_exp,) + exp_b_down.shape[2:])
    h = None
    for i in range(depth):
        kind, occ = i % n_mixers, i // n_mixers
        mod = mod_all[i]
        if kind != 1 and h is None:
            h = _norm_mod(x2, norm_mix[i], mod, seq)
        if kind == 0:
            g = _sc_in(h, sc_w_in[occ].astype(BF16), sc_w_conv[occ], seq)
            x1 = _proj_res(g, sc_w_out[occ].astype(BF16), x2, mod, seq)
        elif kind == 1:
            x1 = _pool_mixer(x2, norm_mix[i], mod, pool_w[occ].astype(BF16), pool_scale[occ], seq)
        elif kind == 2:
            u = _conf_in(h, cf_w_pw1[occ].astype(BF16), cf_b_pw1[occ], cf_w_dw[occ], cf_b_dw[occ], seq)
            x1 = _proj_res(u, cf_w_pw2[occ].astype(BF16), x2, mod, seq, bias=cf_b_pw2[occ],
                           ln=(cf_ln_g[occ], cf_ln_b[occ]))
        else:
            qkv = _qkv(h, sb_w_qkv[occ].astype(BF16), sb_q_gain[occ], sb_k_gain[occ], seq)
            o = _attention(qkv, bsz, seq)
            x1 = _proj_res(o, sb_w_o[occ].astype(BF16), x2, mod, seq)
        nxt = i + 1
        wants_h = nxt < depth and nxt % n_mixers != 1
        x2, h = _routed_experts(
            x1, norm_ffn[i], mod, router_w[i], router_b[i], i, w_gu, b_gu, w_down, b_down, seq,
            norm_mix[nxt] if wants_h else None, mod_all[nxt] if wants_h else None)
    return x2.reshape(bsz, seq, d)
```
